```python
import jax, jax.numpy as jnp
from jax import lax
import numpy as np

D_MODEL = 1024
BATCH = 4
SEQ = 4096
DEPTH = 1

CONV_DIM = D_MODEL // 2
CONV_WIDTH = 3
RWKV_DIM = D_MODEL - CONV_DIM
HEAD_DIM = 64
RWKV_HEADS = RWKV_DIM // HEAD_DIM
DECAY_RANK = 64
ICLR_RANK = 64
GATE_RANK = 160
D_FF = 4 * D_MODEL
PLE_DIM = 256
CONV_COLS = 3 * CONV_DIM
RWKV_COLS = 3 * RWKV_DIM + DECAY_RANK + ICLR_RANK + GATE_RANK
IN_COLS = CONV_COLS + RWKV_COLS
RWKV_SPLITS = (RWKV_DIM, 2 * RWKV_DIM, 3 * RWKV_DIM, 3 * RWKV_DIM + DECAY_RANK, 3 * RWKV_DIM + DECAY_RANK + ICLR_RANK)
RMS_EPS = 1e-6
GN_EPS = 64e-5
L2_EPS = 1e-12

kernel_name = 'hybrid_shortconv_rwkv7_block'


def _rms_norm(h, g):
    hf = h.astype(jnp.float32)
    y = hf * lax.rsqrt(jnp.mean(hf * hf, axis=-1, keepdims=True) + RMS_EPS)
    return (y * g.astype(jnp.float32)).astype(h.dtype)


def _shift_one(u):
    return jnp.pad(u, ((0, 0), (1, 0), (0, 0)))[:, :-1]


def _short_gated_conv(cols, conv_w):
    gate_b, gate_c, hx = jnp.split(cols, 3, axis=-1)
    u = gate_c * hx
    t_len = u.shape[1]
    up = jnp.pad(u, ((0, 0), (CONV_WIDTH - 1, 0), (0, 0)))
    conv = up[:, 0:t_len] * conv_w[0]
    for j in range(1, CONV_WIDTH):
        conv = conv + up[:, j:j + t_len] * conv_w[j]
    return gate_b * conv


def _rwkv7_recurrence(r, w, k, v, a, b):
    bsz, _, n_heads, n = r.shape

    def step(s, inp):
        r_t, w_t, k_t, v_t, a_t, b_t = inp
        sa = jnp.einsum('bhvk,bhk->bhv', s, a_t)
        s = s * w_t[:, :, None, :] + sa[..., None] * b_t[:, :, None, :] + v_t[..., None] * k_t[:, :, None, :]
        return s, jnp.einsum('bhvk,bhk->bhv', s, r_t)

    xs = tuple(jnp.moveaxis(t, 1, 0) for t in (r, w, k, v, a, b))
    s0 = jnp.zeros((bsz, n_heads, n, n), jnp.float32)
    _, ys = lax.scan(step, s0, xs)
    return jnp.moveaxis(ys, 0, 1)


def _rwkv7_time_mix(cols, shift_mu, w_lora_up, w0, a_lora_up, a0, g_lora_up, k_k, k_a, r_k, ln_x_g, ln_x_b):
    f32 = jnp.float32
    bsz, t_len, _ = cols.shape
    u = cols + shift_mu * (_shift_one(cols) - cols)
    r, k, v, xw, xa, xg = jnp.split(u, RWKV_SPLITS, axis=-1)
    w_log = -jax.nn.softplus(-(w0 + jnp.tanh(xw) @ w_lora_up).astype(f32)) - 0.5
    decay = jnp.exp(-jnp.exp(w_log))
    iclr = jax.nn.sigmoid((a0 + xa @ a_lora_up).astype(f32))
    g = jax.nn.sigmoid(xg) @ g_lora_up

    def heads(t):
        return t.astype(f32).reshape(bsz, t_len, RWKV_HEADS, HEAD_DIM)

    kk = heads(k * k_k)
    kk = kk / jnp.maximum(jnp.sqrt(jnp.sum(kk * kk, axis=-1, keepdims=True)), L2_EPS)
    k_h = heads(k.astype(f32) * (1.0 + (iclr - 1.0) * k_a.astype(f32)))
    r_h, v_h, a_h = heads(r), heads(v), heads(iclr)
    y = _rwkv7_recurrence(r_h, heads(decay), k_h, v_h, -kk, kk * a_h)
    mu = jnp.mean(y, axis=-1, keepdims=True)
    var = jnp.mean(jnp.square(y - mu), axis=-1, keepdims=True)
    y = ((y - mu) * lax.rsqrt(var + GN_EPS)).reshape(bsz, t_len, RWKV_DIM)
    y = y * ln_x_g.astype(f32) + ln_x_b.astype(f32)
    bonus = jnp.sum(r_h * k_h * r_k.astype(f32), axis=-1, keepdims=True) * v_h
    y = (y + bonus.reshape(bsz, t_len, RWKV_DIM)) * g.astype(f32)
    return y.astype(cols.dtype)


def setup_inputs(seed: int = 0) -> dict:
    key = jax.random.key(seed)
    ks = jax.random.split(key, 26)
    f32 = jnp.float32
    L, D = DEPTH, D_MODEL

    def nrm(k, shape, scale):
        return jax.random.normal(k, shape, f32) * scale

    return {
        'x': nrm(ks[0], (BATCH, SEQ, D), 1.0),
        'p': nrm(ks[1], (DEPTH, BATCH, SEQ, PLE_DIM), 1.0),
        'norm_mix_g': 1.0 + nrm(ks[2], (L, D), 0.02),
        'w_in': nrm(ks[3], (L, D, IN_COLS), D ** -0.5),
        'conv_w': nrm(ks[4], (L, CONV_WIDTH, CONV_DIM), CONV_WIDTH ** -0.5),
        'shift_mu': jax.random.uniform(ks[5], (L, RWKV_COLS), f32),
        'w_lora_up': nrm(ks[6], (L, DECAY_RANK, RWKV_DIM), DECAY_RANK ** -0.5),
        'w0': jax.random.uniform(ks[7], (L, RWKV_DIM), f32, -4.0, 1.0),
        'a_lora_up': nrm(ks[8], (L, ICLR_RANK, RWKV_DIM), ICLR_RANK ** -0.5),
        'a0': nrm(ks[9], (L, RWKV_DIM), 0.1),
        'g_lora_up': nrm(ks[10], (L, GATE_RANK, RWKV_DIM), GATE_RANK ** -0.5),
        'k_k': 0.85 + nrm(ks[11], (L, RWKV_DIM), 0.02),
        'k_a': 1.0 + nrm(ks[12], (L, RWKV_DIM), 0.02),
        'r_k': nrm(ks[13], (L, RWKV_HEADS, HEAD_DIM), 0.1),
        'ln_x_g': 1.0 + nrm(ks[14], (L, RWKV_DIM), 0.02),
        'ln_x_b': nrm(ks[15], (L, RWKV_DIM), 0.02),
        'w_out': nrm(ks[16], (L, D, D), D ** -0.5),
        'norm_mlp_g': 1.0 + nrm(ks[17], (L, D), 0.02),
        'w_up': nrm(ks[18], (L, D, D_FF), D ** -0.5),
        'w_down': nrm(ks[19], (L, D_FF, D), D_FF ** -0.5),
        'norm_ple_g': 1.0 + nrm(ks[20], (L, D), 0.02),
        'w_ple_gate': nrm(ks[21], (L, D, D), D ** -0.5),
        'w_ple_proj': nrm(ks[22], (L, PLE_DIM, D), PLE_DIM ** -0.5),
        'norm_final_g': 1.0 + nrm(ks[23], (D,), 0.02),
    }


def reference(x, p, norm_mix_g, w_in, conv_w, shift_mu, w_lora_up, w0, a_lora_up, a0, g_lora_up, k_k, k_a, r_k, ln_x_g, ln_x_b, w_out, norm_mlp_g, w_up, w_down, norm_ple_g, w_ple_gate, w_ple_proj, norm_final_g):
    h = x
    for i in range(DEPTH):
        proj = _rms_norm(h, norm_mix_g[i]) @ w_in[i]
        y_conv = _short_gated_conv(proj[..., :CONV_COLS], conv_w[i])
        y_rwkv = _rwkv7_time_mix(proj[..., CONV_COLS:], shift_mu[i], w_lora_up[i], w0[i], a_lora_up[i], a0[i], g_lora_up[i], k_k[i], k_a[i], r_k[i], ln_x_g[i], ln_x_b[i])
        h = h + jnp.concatenate([y_conv, y_rwkv], axis=-1) @ w_out[i]
        hidden = jax.nn.relu(_rms_norm(h, norm_mlp_g[i]) @ w_up[i])
        h = h + jnp.square(hidden) @ w_down[i]
        gate = jax.nn.sigmoid(_rms_norm(h, norm_ple_g[i]) @ w_ple_gate[i])
        h = h + gate * (p[i] @ w_ple_proj[i])
    return _rms_norm(h, norm_final_g)
```

```python
import functools
import math

import numpy as np
import jax
import jax.numpy as jnp
from jax import lax
from jax.experimental import pallas as pl
from jax.experimental.pallas import tpu as pltpu

F32 = jnp.float32
BF16 = jnp.bfloat16

HEAD_DIM = 64
CHUNK = 64
LANES = 128
PAIR = 2 * HEAD_DIM
RMS_EPS = 1e-6
GN_EPS = 64e-5
L2_EPS = 1e-12
DECAY_SCALE = math.exp(-0.5)
CARRY_ROWS = 8

TM_PREP = 256
TM_RWKV = 256
TM_TAIL = 256
VMEM_LIMIT = 56 * 1024 * 1024


def _bf(x):
    return x.astype(BF16)


def _split2(x):
    hi = x.astype(BF16)
    lo = (x - hi.astype(F32)).astype(BF16)
    return hi, lo


def _split3(x):
    h1 = x.astype(BF16)
    r1 = x - h1.astype(F32)
    h2 = r1.astype(BF16)
    h3 = (r1 - h2.astype(F32)).astype(BF16)
    return h1, h2, h3


def _dot(a, b):
    return jnp.dot(a, b, preferred_element_type=F32)


def _dot_nt(a, b):
    return lax.dot_general(a, b, (((1,), (1,)), ((), ())), preferred_element_type=F32)


def _dot3(x, y):
    xh, xl = _split2(x)
    yh, yl = _split2(y)
    return _dot(xh, yh) + (_dot(xh, yl) + _dot(xl, yh))


def _dot_exact_rhs(m, x):
    h1, h2, h3 = _split3(x)
    return _dot(m, h1) + (_dot(m, h2) + _dot(m, h3))


def _headsum(x, bd):
    xh, xl = _split2(x)
    return _dot(xh, bd) + _dot(xl, bd)


def _rms(x, g):
    return x * lax.rsqrt(jnp.mean(x * x, axis=-1, keepdims=True) + RMS_EPS) * g


def _shift_rows(x, carry, n):
    rows = lax.broadcasted_iota(jnp.int32, x.shape, 0)
    out = pltpu.roll(x, n, 0)
    for q in range(n):
        src = CARRY_ROWS - n + q
        out = jnp.where(rows == q, carry[src:src + 1, :], out)
    return out


def _prep_kernel(x_ref, g_ref, win_ref, convw_ref, mu_ref, ww_ref, w0_ref, wa_ref,
                 a0_ref, wg_ref, kkw_ref, ka_ref, bd_ref,
                 yconv_ref, r_ref, k_ref, v_ref, lw_ref, kk_ref, bb_ref, gate_ref,
                 ucar_ref, ccar_ref, *, tiles_per_seq, conv_dim, rwkv_dim):
    i = pl.program_id(0)

    @pl.when(i % tiles_per_seq == 0)
    def _():
        ucar_ref[...] = jnp.zeros_like(ucar_ref)
        ccar_ref[...] = jnp.zeros_like(ccar_ref)

    tm = x_ref.shape[0]
    xb = _bf(_rms(x_ref[...], g_ref[...]))

    c3 = 3 * conv_dim
    pa = _dot(xb, win_ref[:, 0:c3])
    gate_b = pa[:, 0:conv_dim]
    u = pa[:, conv_dim:2 * conv_dim] * pa[:, 2 * conv_dim:c3]
    ucar = ucar_ref[...]
    cw = convw_ref[...]
    conv = (_shift_rows(u, ucar, 2) * cw[0:1, :] + _shift_rows(u, ucar, 1) * cw[1:2, :]
            + u * cw[2:3, :])
    yconv_ref[...] = gate_b * conv
    ucar_ref[...] = u[tm - CARRY_ROWS:tm, :]

    pb = _dot(xb, win_ref[:, c3:])
    prev = _shift_rows(pb, ccar_ref[...], 1)
    uu = pb + mu_ref[...] * (prev - pb)
    ccar_ref[...] = pb[tm - CARRY_ROWS:tm, :]

    d = rwkv_dim
    r = uu[:, 0:d]
    k = uu[:, d:2 * d]
    v = uu[:, 2 * d:3 * d]
    xw = uu[:, 3 * d:3 * d + LANES]
    xa = uu[:, 3 * d + LANES:3 * d + 2 * LANES]
    xg = uu[:, 3 * d + 2 * LANES:]

    z = w0_ref[...] + _dot(_bf(jnp.tanh(xw)), ww_ref[...])
    lw = -DECAY_SCALE * jax.nn.sigmoid(z)
    iclr = jax.nn.sigmoid(a0_ref[...] + _dot(_bf(xa), wa_ref[...]))
    gate = _dot(_bf(jax.nn.sigmoid(xg)), wg_ref[...])

    kk = k * kkw_ref[...]
    nrm = jnp.sqrt(_headsum(kk * kk, bd_ref[...]))
    kk = kk / jnp.maximum(nrm, L2_EPS)

    r_ref[...] = r
    k_ref[...] = k * (1.0 + (iclr - 1.0) * ka_ref[...])
    v_ref[...] = v
    lw_ref[...] = lw
    kk_ref[...] = kk
    bb_ref[...] = kk * iclr
    gate_ref[...] = gate


def _stack(x, m0):
    zero = jnp.zeros_like(x)
    return jnp.concatenate([jnp.where(m0, x, zero), jnp.where(m0, zero, x)], axis=0)


def _fold(x_st):
    return x_st[0:CHUNK, :] + x_st[CHUNK:PAIR, :]


def _rwkv_kernel(r_ref, k_ref, v_ref, lw_ref, kk_ref, bb_ref, gate_ref,
                 tril_ref, triu_ref, bd_ref, rk_ref, lng_ref, lnb_ref,
                 out_ref,
                 s_ref, at_ref, rt_ref, bt_ref, kt_ref, bh_ref, kh_ref, pl_ref, y_ref):
    @pl.when(pl.program_id(1) == 0)
    def _():
        s_ref[...] = jnp.zeros_like(s_ref)

    tm, dim = r_ref.shape
    n_pairs = dim // PAIR
    n_chunks = tm // CHUNK

    lw = lw_ref[...]
    c = _dot_exact_rhs(tril_ref[...], lw)
    dsuf = _dot_exact_rhs(triu_ref[...], lw)
    enc = jnp.exp(-c)
    ed = jnp.exp(dsuf)
    kmod = k_ref[...]
    bb = bb_ref[...]
    at_ref[...] = -(kk_ref[...] * jnp.exp(c - lw))
    rt_ref[...] = r_ref[...] * jnp.exp(c)
    bt_ref[...] = bb * enc
    kt_ref[...] = kmod * enc
    bh_ref[...] = bb * ed
    kh_ref[...] = kmod * ed
    pl_ref[...] = jnp.exp(c + dsuf)

    ri = lax.broadcasted_iota(jnp.int32, (PAIR, PAIR), 0)
    ci = lax.broadcasted_iota(jnp.int32, (PAIR, PAIR), 1)
    same = (ri // CHUNK) == (ci // CHUNK)
    m_strict = same & (ci < ri)
    m_incl = same & (ci <= ri)
    eye = (ri == ci).astype(F32)
    same8 = (ri // 8) == (ci // 8)
    off_masks = [((ri // (2 * s)) == (ci // (2 * s))) & ((ri // s) != (ci // s)) for s in (8, 16, 32)]
    m0 = lax.broadcasted_iota(jnp.int32, (CHUNK, PAIR), 1) < HEAD_DIM
    zeros_pp = jnp.zeros((PAIR, PAIR), F32)

    def chunk_body(ch, carry):
        rows = pl.ds(pl.multiple_of(ch * CHUNK, CHUNK), CHUNK)
        for p in range(n_pairs):
            lanes = slice(PAIR * p, PAIR * (p + 1))
            at = at_ref[rows, lanes]
            rt = rt_ref[rows, lanes]
            bt = bt_ref[rows, lanes]
            kt = kt_ref[rows, lanes]
            vv = v_ref[rows, lanes]
            s0 = s_ref[p]

            lhs = _bf(jnp.concatenate([_stack(at, m0), _stack(rt, m0)], axis=0))
            rhs = _bf(jnp.concatenate([bt, bt, kt, kt], axis=0))
            g = _dot_nt(lhs, rhs)
            a_ab = jnp.where(m_strict, g[0:PAIR, 0:PAIR], zeros_pp)
            a_ak = jnp.where(m_strict, g[0:PAIR, PAIR:], zeros_pp)
            a_rb = jnp.where(m_incl, g[PAIR:, 0:PAIR], zeros_pp)
            a_rk = jnp.where(m_incl, g[PAIR:, PAIR:], zeros_pp)

            nd = jnp.where(same8, a_ab, zeros_pp)
            t = eye + nd
            pw = nd
            for _ in range(2):
                pw = _dot3(pw, pw)
                t = t + _dot3(pw, t)
            for m_off in off_masks:
                t = t + _dot3(t, _dot3(jnp.where(m_off, a_ab, zeros_pp), t))

            sh, sl = _split2(s0)
            xin = _bf(jnp.concatenate([at, rt], axis=0))
            x0 = _dot_nt(xin, sh) + _dot_nt(xin, sl)
            v_st = _stack(vv, m0)
            rhs_u = _stack(x0[0:CHUNK, :], m0) + _dot(_bf(a_ak), _bf(v_st))
            th, tl = _split2(t)
            rb = _bf(rhs_u)
            u_st = _dot(th, rb) + _dot(tl, rb)
            y_st = _stack(x0[CHUNK:, :], m0) + _dot(
                _bf(jnp.concatenate([a_rb, a_rk], axis=1)),
                _bf(jnp.concatenate([u_st, v_st], axis=0)))
            y_ref[rows, lanes] = _fold(y_st)

            uv = jnp.concatenate([_fold(u_st), vv], axis=0)
            bk = jnp.concatenate([bh_ref[rows, lanes], kh_ref[rows, lanes]], axis=0)
            upd = _dot(_bf(uv.T), _bf(bk))
            decay = pl_ref[pl.ds(pl.multiple_of(ch * CHUNK, CHUNK), 1), lanes]
            s_ref[p] = s0 * decay + jnp.where(same, upd, zeros_pp)
        return carry

    lax.fori_loop(0, n_chunks, chunk_body, 0)

    bd = bd_ref[...]
    inv_n = 1.0 / HEAD_DIM
    y = y_ref[...]
    mu = _headsum(y, bd) * inv_n
    yc = y - mu
    var = _headsum(yc * yc, bd) * inv_n
    yn = yc * lax.rsqrt(var + GN_EPS) * lng_ref[...] + lnb_ref[...]
    bonus = _headsum(r_ref[...] * k_ref[...] * rk_ref[...], bd) * v_ref[...]
    out_ref[...] = (yn + bonus) * gate_ref[...]


def _tail_kernel(x_ref, yc_ref, yr_ref, p_ref, wout_ref, gmlp_ref, wup_ref, wdown_ref,
                 gple_ref, wgate_ref, wple_ref, gfin_ref, out_ref, *, conv_dim, ff_block, final_norm):
    mix = _dot(_bf(yc_ref[...]), wout_ref[0:conv_dim, :]) + _dot(_bf(yr_ref[...]), wout_ref[conv_dim:, :])
    h = x_ref[...] + mix

    hn = _bf(_rms(h, gmlp_ref[...]))
    d_ff = wup_ref.shape[1]
    acc = None
    for j in range(d_ff // ff_block):
        cols = slice(j * ff_block, (j + 1) * ff_block)
        hid = jnp.maximum(_dot(hn, wup_ref[:, cols]), 0.0)
        part = _dot(_bf(hid * hid), wdown_ref[cols, :])
        acc = part if acc is None else acc + part
    h = h + acc

    gate = jax.nn.sigmoid(_dot(_bf(_rms(h, gple_ref[...])), wgate_ref[...]))
    h = h + gate * _dot(_bf(p_ref[...]), wple_ref[...])
    if final_norm:
        h = _rms(h, gfin_ref[...])
    out_ref[...] = h


def _const_spec(shape):
    nd = len(shape)
    return pl.BlockSpec(shape, lambda *_: (0,) * nd)


def _pad_cols(w, n):
    return jnp.pad(w, ((0, 0), (0, n - w.shape[1])))


def _pad_rows(w, n):
    return jnp.pad(w, ((0, n - w.shape[0]), (0, 0)))


def _block_diag_ones(n, blk):
    idx = np.arange(n) // blk
    return jnp.asarray(idx[:, None] == idx[None, :], dtype=BF16)


def _chunk_tri(n, blk, upper_strict):
    t = np.arange(n)
    same = (t[:, None] // blk) == (t[None, :] // blk)
    m = same & ((t[None, :] > t[:, None]) if upper_strict else (t[None, :] <= t[:, None]))
    return jnp.asarray(m, dtype=BF16)


def kernel(x, p, norm_mix_g, w_in, conv_w, shift_mu, w_lora_up, w0, a_lora_up, a0, g_lora_up, k_k, k_a, r_k, ln_x_g, ln_x_b, w_out, norm_mlp_g, w_up, w_down, norm_ple_g, w_ple_gate, w_ple_proj, norm_final_g):
    bsz, seq, d_model = x.shape
    depth = w_in.shape[0]
    n = bsz * seq
    rwkv_dim = w_lora_up.shape[2]
    conv_dim = conv_w.shape[2]
    decay_rank = w_lora_up.shape[1]
    iclr_rank = a_lora_up.shape[1]
    gate_rank = g_lora_up.shape[1]
    c3 = 3 * conv_dim
    r3 = c3 + 3 * rwkv_dim
    assert seq % TM_PREP == 0 and seq % TM_RWKV == 0 and n % TM_TAIL == 0
    assert rwkv_dim % PAIR == 0 and decay_rank <= LANES and iclr_rank <= LANES and gate_rank <= 2 * LANES
    rwkv_cols = 3 * rwkv_dim + 4 * LANES

    bd = _block_diag_ones(rwkv_dim, HEAD_DIM)
    tril = _chunk_tri(TM_RWKV, CHUNK, upper_strict=False)
    triu = _chunk_tri(TM_RWKV, CHUNK, upper_strict=True)
    row = lambda a: a.reshape(1, -1).astype(F32)

    h = x.reshape(n, d_model)
    for li in range(depth):
        wi = w_in[li]
        s0, s1, s2 = r3, r3 + decay_rank, r3 + decay_rank + iclr_rank
        win_p = jnp.concatenate([
            wi[:, :s0], _pad_cols(wi[:, s0:s1], LANES), _pad_cols(wi[:, s1:s2], LANES),
            _pad_cols(wi[:, s2:], 2 * LANES)], axis=1).astype(BF16)
        mu = shift_mu[li].reshape(1, -1)
        o = 3 * rwkv_dim
        mu_p = jnp.concatenate([
            mu[:, :o], _pad_cols(mu[:, o:o + decay_rank], LANES),
            _pad_cols(mu[:, o + decay_rank:o + decay_rank + iclr_rank], LANES),
            _pad_cols(mu[:, o + decay_rank + iclr_rank:], 2 * LANES)], axis=1)
        ww = _pad_rows(w_lora_up[li], LANES).astype(BF16)
        wa = _pad_rows(a_lora_up[li], LANES).astype(BF16)
        wg = _pad_rows(g_lora_up[li], 2 * LANES).astype(BF16)

        row_spec = lambda w: pl.BlockSpec((TM_PREP, w), lambda i: (i, 0))
        stream = jax.ShapeDtypeStruct((n, rwkv_dim), F32)
        prep_out = pl.pallas_call(
            functools.partial(_prep_kernel, tiles_per_seq=seq // TM_PREP, conv_dim=conv_dim, rwkv_dim=rwkv_dim),
            grid=(n // TM_PREP,),
            in_specs=[
                row_spec(d_model), _const_spec((1, d_model)), _const_spec(win_p.shape),
                _const_spec((conv_w.shape[1], conv_dim)), _const_spec((1, rwkv_cols)),
                _const_spec(ww.shape), _const_spec((1, rwkv_dim)), _const_spec(wa.shape),
                _const_spec((1, rwkv_dim)), _const_spec(wg.shape), _const_spec((1, rwkv_dim)),
                _const_spec((1, rwkv_dim)), _const_spec(bd.shape)],
            out_specs=[row_spec(conv_dim)] + [row_spec(rwkv_dim)] * 7,
            out_shape=[jax.ShapeDtypeStruct((n, conv_dim), F32)] + [stream] * 7,
            scratch_shapes=[pltpu.VMEM((CARRY_ROWS, conv_dim), F32), pltpu.VMEM((CARRY_ROWS, rwkv_cols), F32)],
            compiler_params=pltpu.CompilerParams(dimension_semantics=("arbitrary",), vmem_limit_bytes=VMEM_LIMIT),
            name="prep",
        )(h, row(norm_mix_g[li]), win_p, conv_w[li], mu_p, ww, row(w0[li]), wa, row(a0[li]), wg,
          row(k_k[li]), row(k_a[li]), bd)
        y_conv, r_s, k_s, v_s, lw_s, kk_s, bb_s, gate_s = prep_out

        tiles = seq // TM_RWKV
        seq_spec = pl.BlockSpec((TM_RWKV, rwkv_dim), lambda b, j: (b * tiles + j, 0))
        tile_scr = pltpu.VMEM((TM_RWKV, rwkv_dim), F32)
        y_rwkv = pl.pallas_call(
            _rwkv_kernel,
            grid=(bsz, tiles),
            in_specs=[seq_spec] * 7 + [
                _const_spec(tril.shape), _const_spec(triu.shape), _const_spec(bd.shape),
                _const_spec((1, rwkv_dim)), _const_spec((1, rwkv_dim)), _const_spec((1, rwkv_dim))],
            out_specs=seq_spec,
            out_shape=stream,
            scratch_shapes=[pltpu.VMEM((rwkv_dim // PAIR, PAIR, PAIR), F32)] + [tile_scr] * 8,
            compiler_params=pltpu.CompilerParams(dimension_semantics=("arbitrary", "arbitrary"), vmem_limit_bytes=VMEM_LIMIT),
            name="rwkv",
        )(r_s, k_s, v_s, lw_s, kk_s, bb_s, gate_s, tril, triu, bd,
          row(r_k[li]), row(ln_x_g[li]), row(ln_x_b[li]))

        tail_row = lambda w: pl.BlockSpec((TM_TAIL, w), lambda i: (i, 0))
        ple_dim = p.shape[-1]
        d_ff = w_up.shape[2]
        h = pl.pallas_call(
            functools.partial(_tail_kernel, conv_dim=conv_dim, ff_block=1024, final_norm=(li == depth - 1)),
            grid=(n // TM_TAIL,),
            in_specs=[
                tail_row(d_model), tail_row(conv_dim), tail_row(rwkv_dim), tail_row(ple_dim),
                _const_spec((d_model, d_model)), _const_spec((1, d_model)), _const_spec((d_model, d_ff)),
                _const_spec((d_ff, d_model)), _const_spec((1, d_model)), _const_spec((d_model, d_model)),
                _const_spec((ple_dim, d_model)), _const_spec((1, d_model))],
            out_specs=tail_row(d_model),
            out_shape=jax.ShapeDtypeStruct((n, d_model), F32),
            compiler_params=pltpu.CompilerParams(dimension_semantics=("parallel",), vmem_limit_bytes=VMEM_LIMIT),
            name="tail",
        )(h, y_conv, y_rwkv, p[li].reshape(n, ple_dim), w_out[li].astype(BF16), row(norm_mlp_g[li]),
          w_up[li].astype(BF16), w_down[li].astype(BF16), row(norm_ple_g[li]),
          w_ple_gate[li].astype(BF16), w_ple_proj[li].astype(BF16), row(norm_final_g))
    return h.reshape(bsz, seq, d_model)
```

```python
import functools
import math

import numpy as np
import jax
import jax.numpy as jnp
from jax import lax
from jax.experimental import pallas as pl
from jax.experimental.pallas import tpu as pltpu

F32 = jnp.float32
BF16 = jnp.bfloat16

HEAD_DIM = 64
CHUNK = 64
LANES = 128
PAIR = 2 * HEAD_DIM
RMS_EPS = 1e-6
GN_EPS = 64e-5
L2_EPS = 1e-12
DECAY_SCALE = math.exp(-0.5)
CARRY_ROWS = 8

TM_PREP = 256
TM_RWKV = 256
TM_TAIL = 256
VMEM_LIMIT = 56 * 1024 * 1024


def _bf(x):
    return x.astype(BF16)


def _split2(x):
    hi = x.astype(BF16)
    lo = (x - hi.astype(F32)).astype(BF16)
    return hi, lo


def _split3(x):
    h1 = x.astype(BF16)
    r1 = x - h1.astype(F32)
    h2 = r1.astype(BF16)
    h3 = (r1 - h2.astype(F32)).astype(BF16)
    return h1, h2, h3


def _dot(a, b):
    return jnp.dot(a, b, preferred_element_type=F32)


def _dot_nt(a, b):
    return lax.dot_general(a, b, (((1,), (1,)), ((), ())), preferred_element_type=F32)


def _dot3(x, y):
    xh, xl = _split2(x)
    yh, yl = _split2(y)
    return _dot(xh, yh) + (_dot(xh, yl) + _dot(xl, yh))


def _dot_exact_rhs(m, x):
    h1, h2, h3 = _split3(x)
    return _dot(m, h1) + (_dot(m, h2) + _dot(m, h3))


def _headsum(x, bd):
    xh, xl = _split2(x)
    return _dot(xh, bd) + _dot(xl, bd)


def _rms(x, g):
    return x * lax.rsqrt(jnp.mean(x * x, axis=-1, keepdims=True) + RMS_EPS) * g


def _shift_rows(x, carry, n):
    rows = lax.broadcasted_iota(jnp.int32, x.shape, 0)
    out = pltpu.roll(x, n, 0)
    for q in range(n):
        src = CARRY_ROWS - n + q
        out = jnp.where(rows == q, carry[src:src + 1, :], out)
    return out


def _prep_kernel(x_ref, g_ref, win_ref, convw_ref, mu_ref, ww_ref, w0_ref, wa_ref,
                 a0_ref, wg_ref, kkw_ref, ka_ref, bd_ref,
                 yconv_ref, r_ref, k_ref, v_ref, lw_ref, kk_ref, bb_ref, gate_ref,
                 ucar_ref, ccar_ref, *, tiles_per_seq, conv_dim, rwkv_dim):
    i = pl.program_id(0)

    @pl.when(i % tiles_per_seq == 0)
    def _():
        ucar_ref[...] = jnp.zeros_like(ucar_ref)
        ccar_ref[...] = jnp.zeros_like(ccar_ref)

    tm = x_ref.shape[0]
    xb = _bf(_rms(x_ref[...], g_ref[...]))

    c3 = 3 * conv_dim
    pa = _dot(xb, win_ref[:, 0:c3])
    gate_b = pa[:, 0:conv_dim]
    u = pa[:, conv_dim:2 * conv_dim] * pa[:, 2 * conv_dim:c3]
    ucar = ucar_ref[...]
    cw = convw_ref[...]
    conv = (_shift_rows(u, ucar, 2) * cw[0:1, :] + _shift_rows(u, ucar, 1) * cw[1:2, :]
            + u * cw[2:3, :])
    yconv_ref[...] = gate_b * conv
    ucar_ref[...] = u[tm - CARRY_ROWS:tm, :]

    pb = _dot(xb, win_ref[:, c3:])
    prev = _shift_rows(pb, ccar_ref[...], 1)
    uu = pb + mu_ref[...] * (prev - pb)
    ccar_ref[...] = pb[tm - CARRY_ROWS:tm, :]

    d = rwkv_dim
    r = uu[:, 0:d]
    k = uu[:, d:2 * d]
    v = uu[:, 2 * d:3 * d]
    xw = uu[:, 3 * d:3 * d + LANES]
    xa = uu[:, 3 * d + LANES:3 * d + 2 * LANES]
    xg = uu[:, 3 * d + 2 * LANES:]

    z = w0_ref[...] + _dot(_bf(jnp.tanh(xw)), ww_ref[...])
    lw = -DECAY_SCALE * jax.nn.sigmoid(z)
    iclr = jax.nn.sigmoid(a0_ref[...] + _dot(_bf(xa), wa_ref[...]))
    gate = _dot(_bf(jax.nn.sigmoid(xg)), wg_ref[...])

    kk = k * kkw_ref[...]
    nrm = jnp.sqrt(_headsum(kk * kk, bd_ref[...]))
    kk = kk / jnp.maximum(nrm, L2_EPS)

    r_ref[...] = r
    k_ref[...] = k * (1.0 + (iclr - 1.0) * ka_ref[...])
    v_ref[...] = v
    lw_ref[...] = lw
    kk_ref[...] = kk
    bb_ref[...] = kk * iclr
    gate_ref[...] = gate


def _stack(x, m0):
    zero = jnp.zeros_like(x)
    return jnp.concatenate([jnp.where(m0, x, zero), jnp.where(m0, zero, x)], axis=0)


def _fold(x_st):
    return x_st[0:CHUNK, :] + x_st[CHUNK:PAIR, :]


def _rwkv_kernel(r_ref, k_ref, v_ref, lw_ref, kk_ref, bb_ref, gate_ref,
                 tril_ref, triu_ref, bd_ref, rk_ref, lng_ref, lnb_ref,
                 out_ref,
                 s_ref, at_ref, rt_ref, bt_ref, kt_ref, bh_ref, kh_ref, pl_ref, y_ref):
    @pl.when(pl.program_id(1) == 0)
    def _():
        s_ref[...] = jnp.zeros_like(s_ref)

    tm, dim = r_ref.shape
    n_pairs = dim // PAIR
    n_chunks = tm // CHUNK

    lw = lw_ref[...]
    c = _dot_exact_rhs(tril_ref[...], lw)
    dsuf = _dot_exact_rhs(triu_ref[...], lw)
    enc = jnp.exp(-c)
    ed = jnp.exp(dsuf)
    kmod = k_ref[...]
    bb = bb_ref[...]
    at_ref[...] = -(kk_ref[...] * jnp.exp(c - lw))
    rt_ref[...] = r_ref[...] * jnp.exp(c)
    bt_ref[...] = bb * enc
    kt_ref[...] = kmod * enc
    bh_ref[...] = bb * ed
    kh_ref[...] = kmod * ed
    pl_ref[...] = jnp.exp(c + dsuf)

    ri = lax.broadcasted_iota(jnp.int32, (PAIR, PAIR), 0)
    ci = lax.broadcasted_iota(jnp.int32, (PAIR, PAIR), 1)
    same = (ri // CHUNK) == (ci // CHUNK)
    m_strict = same & (ci < ri)
    m_incl = same & (ci <= ri)
    eye = (ri == ci).astype(F32)
    same8 = (ri // 8) == (ci // 8)
    off_masks = [((ri // (2 * s)) == (ci // (2 * s))) & ((ri // s) != (ci // s)) for s in (8, 16, 32)]
    m0 = lax.broadcasted_iota(jnp.int32, (CHUNK, PAIR), 1) < HEAD_DIM
    zeros_pp = jnp.zeros((PAIR, PAIR), F32)

    def dot3_each(xs, ys):
        return [_dot3(x, y) for x, y in zip(xs, ys)]

    def add_each(xs, ys):
        return [x + y for x, y in zip(xs, ys)]

    def pairs_chunk(args):
        at, rt, bt, kt, vv, bh, kh, s0, decay = [list(a) for a in zip(*args)]
        pairs = range(len(at))
        g = [_dot_nt(_bf(jnp.concatenate([_stack(at[p], m0), _stack(rt[p], m0)], axis=0)),
                     _bf(jnp.concatenate([bt[p], bt[p], kt[p], kt[p]], axis=0))) for p in pairs]
        a_ab = [jnp.where(m_strict, g[p][0:PAIR, 0:PAIR], zeros_pp) for p in pairs]
        a_ak = [jnp.where(m_strict, g[p][0:PAIR, PAIR:], zeros_pp) for p in pairs]
        a_rb = [jnp.where(m_incl, g[p][PAIR:, 0:PAIR], zeros_pp) for p in pairs]
        a_rk = [jnp.where(m_incl, g[p][PAIR:, PAIR:], zeros_pp) for p in pairs]

        pw = [jnp.where(same8, a, zeros_pp) for a in a_ab]
        t = [eye + n for n in pw]
        for _ in range(2):
            pw = dot3_each(pw, pw)
            t = add_each(t, dot3_each(pw, t))
        for m_off in off_masks:
            ct = dot3_each([jnp.where(m_off, a, zeros_pp) for a in a_ab], t)
            t = add_each(t, dot3_each(t, ct))

        x0 = []
        for p in pairs:
            sh, sl = _split2(s0[p])
            xin = _bf(jnp.concatenate([at[p], rt[p]], axis=0))
            x0.append(_dot_nt(xin, sh) + _dot_nt(xin, sl))
        v_st = [_stack(v, m0) for v in vv]
        rhs_u = [_stack(x0[p][0:CHUNK, :], m0) + _dot(_bf(a_ak[p]), _bf(v_st[p])) for p in pairs]
        u_st = []
        for p in pairs:
            th, tl = _split2(t[p])
            rb = _bf(rhs_u[p])
            u_st.append(_dot(th, rb) + _dot(tl, rb))
        y_st = [_stack(x0[p][CHUNK:, :], m0) + _dot(
            _bf(jnp.concatenate([a_rb[p], a_rk[p]], axis=1)),
            _bf(jnp.concatenate([u_st[p], v_st[p]], axis=0))) for p in pairs]
        upd = [_dot(_bf(jnp.concatenate([_fold(u_st[p]), vv[p]], axis=0).T),
                    _bf(jnp.concatenate([bh[p], kh[p]], axis=0))) for p in pairs]
        return [(_fold(y_st[p]), s0[p] * decay[p] + jnp.where(same, upd[p], zeros_pp)) for p in pairs]

    def chunk_body(ch, carry):
        start = pl.multiple_of(ch * CHUNK, CHUNK)
        rows = pl.ds(start, CHUNK)
        args = []
        for p in range(n_pairs):
            lanes = slice(PAIR * p, PAIR * (p + 1))
            args.append((at_ref[rows, lanes], rt_ref[rows, lanes], bt_ref[rows, lanes],
                         kt_ref[rows, lanes], v_ref[rows, lanes], bh_ref[rows, lanes],
                         kh_ref[rows, lanes], s_ref[p], pl_ref[pl.ds(start, 1), lanes]))
        outs = pairs_chunk(args)
        for p, (y_nat, s_new) in enumerate(outs):
            y_ref[rows, slice(PAIR * p, PAIR * (p + 1))] = y_nat
            s_ref[p] = s_new
        return carry

    lax.fori_loop(0, n_chunks, chunk_body, 0)

    bd = bd_ref[...]
    inv_n = 1.0 / HEAD_DIM
    y = y_ref[...]
    mu = _headsum(y, bd) * inv_n
    yc = y - mu
    var = _headsum(yc * yc, bd) * inv_n
    yn = yc * lax.rsqrt(var + GN_EPS) * lng_ref[...] + lnb_ref[...]
    bonus = _headsum(r_ref[...] * k_ref[...] * rk_ref[...], bd) * v_ref[...]
    out_ref[...] = (yn + bonus) * gate_ref[...]


def _tail_kernel(x_ref, yc_ref, yr_ref, p_ref, wout_ref, gmlp_ref, wup_ref, wdown_ref,
                 gple_ref, wgate_ref, wple_ref, gfin_ref, out_ref, *, conv_dim, ff_block, final_norm):
    mix = _dot(_bf(yc_ref[...]), wout_ref[0:conv_dim, :]) + _dot(_bf(yr_ref[...]), wout_ref[conv_dim:, :])
    h = x_ref[...] + mix

    hn = _bf(_rms(h, gmlp_ref[...]))
    d_ff = wup_ref.shape[1]
    acc = None
    for j in range(d_ff // ff_block):
        cols = slice(j * ff_block, (j + 1) * ff_block)
        hid = jnp.maximum(_dot(hn, wup_ref[:, cols]), 0.0)
        part = _dot(_bf(hid * hid), wdown_ref[cols, :])
        acc = part if acc is None else acc + part
    h = h + acc

    gate = jax.nn.sigmoid(_dot(_bf(_rms(h, gple_ref[...])), wgate_ref[...]))
    h = h + gate * _dot(_bf(p_ref[...]), wple_ref[...])
    if final_norm:
        h = _rms(h, gfin_ref[...])
    out_ref[...] = h


def _const_spec(shape):
    nd = len(shape)
    return pl.BlockSpec(shape, lambda *_: (0,) * nd)


def _pad_cols(w, n):
    return jnp.pad(w, ((0, 0), (0, n - w.shape[1])))


def _pad_rows(w, n):
    return jnp.pad(w, ((0, n - w.shape[0]), (0, 0)))


def _block_diag_ones(n, blk):
    idx = np.arange(n) // blk
    return jnp.asarray(idx[:, None] == idx[None, :], dtype=BF16)


def _chunk_tri(n, blk, upper_strict):
    t = np.arange(n)
    same = (t[:, None] // blk) == (t[None, :] // blk)
    m = same & ((t[None, :] > t[:, None]) if upper_strict else (t[None, :] <= t[:, None]))
    return jnp.asarray(m, dtype=BF16)


def kernel(x, p, norm_mix_g, w_in, conv_w, shift_mu, w_lora_up, w0, a_lora_up, a0, g_lora_up, k_k, k_a, r_k, ln_x_g, ln_x_b, w_out, norm_mlp_g, w_up, w_down, norm_ple_g, w_ple_gate, w_ple_proj, norm_final_g):
    bsz, seq, d_model = x.shape
    depth = w_in.shape[0]
    n = bsz * seq
    rwkv_dim = w_lora_up.shape[2]
    conv_dim = conv_w.shape[2]
    decay_rank = w_lora_up.shape[1]
    iclr_rank = a_lora_up.shape[1]
    gate_rank = g_lora_up.shape[1]
    c3 = 3 * conv_dim
    r3 = c3 + 3 * rwkv_dim
    assert seq % TM_PREP == 0 and seq % TM_RWKV == 0 and n % TM_TAIL == 0
    assert rwkv_dim % PAIR == 0 and decay_rank <= LANES and iclr_rank <= LANES and gate_rank <= 2 * LANES
    rwkv_cols = 3 * rwkv_dim + 4 * LANES

    bd = _block_diag_ones(rwkv_dim, HEAD_DIM)
    tril = _chunk_tri(TM_RWKV, CHUNK, upper_strict=False)
    triu = _chunk_tri(TM_RWKV, CHUNK, upper_strict=True)
    row = lambda a: a.reshape(1, -1).astype(F32)

    h = x.reshape(n, d_model)
    for li in range(depth):
        wi = w_in[li]
        s0, s1, s2 = r3, r3 + decay_rank, r3 + decay_rank + iclr_rank
        win_p = jnp.concatenate([
            wi[:, :s0], _pad_cols(wi[:, s0:s1], LANES), _pad_cols(wi[:, s1:s2], LANES),
            _pad_cols(wi[:, s2:], 2 * LANES)], axis=1).astype(BF16)
        mu = shift_mu[li].reshape(1, -1)
        o = 3 * rwkv_dim
        mu_p = jnp.concatenate([
            mu[:, :o], _pad_cols(mu[:, o:o + decay_rank], LANES),
            _pad_cols(mu[:, o + decay_rank:o + decay_rank + iclr_rank], LANES),
            _pad_cols(mu[:, o + decay_rank + iclr_rank:], 2 * LANES)], axis=1)
        ww = _pad_rows(w_lora_up[li], LANES).astype(BF16)
        wa = _pad_rows(a_lora_up[li], LANES).astype(BF16)
        wg = _pad_rows(g_lora_up[li], 2 * LANES).astype(BF16)

        row_spec = lambda w: pl.BlockSpec((TM_PREP, w), lambda i: (i, 0))
        stream = jax.ShapeDtypeStruct((n, rwkv_dim), F32)
        prep_out = pl.pallas_call(
            functools.partial(_prep_kernel, tiles_per_seq=seq // TM_PREP, conv_dim=conv_dim, rwkv_dim=rwkv_dim),
            grid=(n // TM_PREP,),
            in_specs=[
                row_spec(d_model), _const_spec((1, d_model)), _const_spec(win_p.shape),
                _const_spec((conv_w.shape[1], conv_dim)), _const_spec((1, rwkv_cols)),
                _const_spec(ww.shape), _const_spec((1, rwkv_dim)), _const_spec(wa.shape),
                _const_spec((1, rwkv_dim)), _const_spec(wg.shape), _const_spec((1, rwkv_dim)),
                _const_spec((1, rwkv_dim)), _const_spec(bd.shape)],
            out_specs=[row_spec(conv_dim)] + [row_spec(rwkv_dim)] * 7,
            out_shape=[jax.ShapeDtypeStruct((n, conv_dim), F32)] + [stream] * 7,
            scratch_shapes=[pltpu.VMEM((CARRY_ROWS, conv_dim), F32), pltpu.VMEM((CARRY_ROWS, rwkv_cols), F32)],
            compiler_params=pltpu.CompilerParams(dimension_semantics=("arbitrary",), vmem_limit_bytes=VMEM_LIMIT),
            name="prep",
        )(h, row(norm_mix_g[li]), win_p, conv_w[li], mu_p, ww, row(w0[li]), wa, row(a0[li]), wg,
          row(k_k[li]), row(k_a[li]), bd)
        y_conv, r_s, k_s, v_s, lw_s, kk_s, bb_s, gate_s = prep_out

        tiles = seq // TM_RWKV
        seq_spec = pl.BlockSpec((TM_RWKV, rwkv_dim), lambda b, j: (b * tiles + j, 0))
        tile_scr = pltpu.VMEM((TM_RWKV, rwkv_dim), F32)
        y_rwkv = pl.pallas_call(
            _rwkv_kernel,
            grid=(bsz, tiles),
            in_specs=[seq_spec] * 7 + [
                _const_spec(tril.shape), _const_spec(triu.shape), _const_spec(bd.shape),
                _const_spec((1, rwkv_dim)), _const_spec((1, rwkv_dim)), _const_spec((1, rwkv_dim))],
            out_specs=seq_spec,
            out_shape=stream,
            scratch_shapes=[pltpu.VMEM((rwkv_dim // PAIR, PAIR, PAIR), F32)] + [tile_scr] * 8,
            compiler_params=pltpu.CompilerParams(dimension_semantics=("arbitrary", "arbitrary"), vmem_limit_bytes=VMEM_LIMIT),
            name="rwkv",
        )(r_s, k_s, v_s, lw_s, kk_s, bb_s, gate_s, tril, triu, bd,
          row(r_k[li]), row(ln_x_g[li]), row(ln_x_b[li]))

        tail_row = lambda w: pl.BlockSpec((TM_TAIL, w), lambda i: (i, 0))
        ple_dim = p.shape[-1]
        d_ff = w_up.shape[2]
        h = pl.pallas_call(
            functools.partial(_tail_kernel, conv_dim=conv_dim, ff_block=1024, final_norm=(li == depth - 1)),
            grid=(n // TM_TAIL,),
            in_specs=[
                tail_row(d_model), tail_row(conv_dim), tail_row(rwkv_dim), tail_row(ple_dim),
                _const_spec((d_model, d_model)), _const_spec((1, d_model)), _const_spec((d_model, d_ff)),
                _const_spec((d_ff, d_model)), _const_spec((1, d_model)), _const_spec((d_model, d_model)),
                _const_spec((ple_dim, d_model)), _const_spec((1, d_model))],
            out_specs=tail_row(d_model),
            out_shape=jax.ShapeDtypeStruct((n, d_model), F32),
            compiler_params=pltpu.CompilerParams(dimension_semantics=("parallel",), vmem_limit_bytes=VMEM_LIMIT),
            name="tail",
        )(h, y_conv, y_rwkv, p[li].reshape(n, ple_dim), w_out[li].astype(BF16), row(norm_mlp_g[li]),
          w_up[li].astype(BF16), w_down[li].astype(BF16), row(norm_ple_g[li]),
          w_ple_gate[li].astype(BF16), w_ple_proj[li].astype(BF16), row(norm_final_g))
    return h.reshape(bsz, seq, d_model)
```

```python
import functools
import math

import numpy as np
import jax
import jax.numpy as jnp
from jax import lax
from jax.experimental import pallas as pl
from jax.experimental.pallas import tpu as pltpu

F32 = jnp.float32
BF16 = jnp.bfloat16

HEAD_DIM = 64
CHUNK = 64
LANES = 128
GROUP_HEADS = 4
GROUP = GROUP_HEADS * HEAD_DIM
RMS_EPS = 1e-6
GN_EPS = 64e-5
L2_EPS = 1e-12
DECAY_SCALE = math.exp(-0.5)
CARRY_ROWS = 8

TM_PREP = 256
TM_RWKV = 128
TM_TAIL = 256
VMEM_LIMIT = 56 * 1024 * 1024


def _bf(x):
    return x.astype(BF16)


def _split2(x):
    hi = x.astype(BF16)
    lo = (x - hi.astype(F32)).astype(BF16)
    return hi, lo


def _dot(a, b):
    return jnp.dot(a, b, preferred_element_type=F32)


def _dot_nt(a, b):
    return lax.dot_general(a, b, (((1,), (1,)), ((), ())), preferred_element_type=F32)


def _headsum(x, bd):
    parts = []
    for g in range(x.shape[1] // GROUP):
        xh, xl = _split2(x[:, g * GROUP:(g + 1) * GROUP])
        parts.append(_dot(xh, bd) + _dot(xl, bd))
    return parts[0] if len(parts) == 1 else jnp.concatenate(parts, axis=1)


def _rms(x, g):
    return x * lax.rsqrt(jnp.mean(x * x, axis=-1, keepdims=True) + RMS_EPS) * g


def _shift_rows(x, carry, n):
    rows = lax.broadcasted_iota(jnp.int32, x.shape, 0)
    out = pltpu.roll(x, n, 0)
    for q in range(n):
        src = CARRY_ROWS - n + q
        out = jnp.where(rows == q, carry[src:src + 1, :], out)
    return out


def _prep_kernel(x_ref, g_ref, win_ref, convw_ref, mu_ref, ww_ref, w0_ref, wa_ref,
                 a0_ref, wg_ref, kkw_ref, ka_ref, bd_ref,
                 yconv_ref, r_ref, k_ref, v_ref, lw_ref, kk_ref, bb_ref, gate_ref,
                 ucar_ref, ccar_ref, *, tiles_per_seq, conv_dim, rwkv_dim):
    i = pl.program_id(0)

    @pl.when(i % tiles_per_seq == 0)
    def _():
        ucar_ref[...] = jnp.zeros_like(ucar_ref)
        ccar_ref[...] = jnp.zeros_like(ccar_ref)

    tm = x_ref.shape[0]
    xb = _bf(_rms(x_ref[...], g_ref[...]))

    c3 = 3 * conv_dim
    pa = _dot(xb, win_ref[:, 0:c3])
    gate_b = pa[:, 0:conv_dim]
    u = pa[:, conv_dim:2 * conv_dim] * pa[:, 2 * conv_dim:c3]
    ucar = ucar_ref[...]
    cw = convw_ref[...]
    conv = (_shift_rows(u, ucar, 2) * cw[0:1, :] + _shift_rows(u, ucar, 1) * cw[1:2, :]
            + u * cw[2:3, :])
    yconv_ref[...] = gate_b * conv
    ucar_ref[...] = u[tm - CARRY_ROWS:tm, :]

    pb = _dot(xb, win_ref[:, c3:])
    prev = _shift_rows(pb, ccar_ref[...], 1)
    uu = pb + mu_ref[...] * (prev - pb)
    ccar_ref[...] = pb[tm - CARRY_ROWS:tm, :]

    d = rwkv_dim
    r = uu[:, 0:d]
    k = uu[:, d:2 * d]
    v = uu[:, 2 * d:3 * d]
    xw = uu[:, 3 * d:3 * d + LANES]
    xa = uu[:, 3 * d + LANES:3 * d + 2 * LANES]
    xg = uu[:, 3 * d + 2 * LANES:]

    z = w0_ref[...] + _dot(_bf(jnp.tanh(xw)), ww_ref[...])
    lw = -DECAY_SCALE * jax.nn.sigmoid(z)
    iclr = jax.nn.sigmoid(a0_ref[...] + _dot(_bf(xa), wa_ref[...]))
    gate = _dot(_bf(jax.nn.sigmoid(xg)), wg_ref[...])

    kk = k * kkw_ref[...]
    nrm = jnp.sqrt(_headsum(kk * kk, bd_ref[...]))
    kk = kk / jnp.maximum(nrm, L2_EPS)

    r_ref[...] = r
    k_ref[...] = k * (1.0 + (iclr - 1.0) * ka_ref[...])
    v_ref[...] = v
    lw_ref[...] = lw
    kk_ref[...] = kk
    bb_ref[...] = kk * iclr
    gate_ref[...] = gate


def _rwkv_kernel(r_ref, k_ref, v_ref, lw_ref, kk_ref, bb_ref, gate_ref,
                 tril_ref, bd_ref, rk_ref, lng_ref, lnb_ref,
                 out_ref, s_ref):
    @pl.when(pl.program_id(0) == 0)
    def _():
        s_ref[...] = jnp.zeros_like(s_ref)

    n_seq, tm, dim = r_ref.shape
    n_groups = dim // GROUP
    n_chunks = tm // CHUNK

    ri = lax.broadcasted_iota(jnp.int32, (CHUNK, GROUP), 0)
    ji = lax.broadcasted_iota(jnp.int32, (CHUNK, GROUP), 1) % HEAD_DIM
    m_strict = ji < ri
    m_incl = ji <= ri
    eye = (ji == ri).astype(F32)
    same8 = (ji // 8) == (ri // 8)
    off_masks = [((ji // (2 * s)) == (ri // (2 * s))) & ((ji // s) != (ri // s)) for s in (8, 16, 32)]
    zeros_c = jnp.zeros((CHUNK, GROUP), F32)
    gr = lax.broadcasted_iota(jnp.int32, (GROUP, GROUP), 0) // HEAD_DIM
    gc = lax.broadcasted_iota(jnp.int32, (GROUP, GROUP), 1) // HEAD_DIM
    bd_mask = gr == gc
    zeros_gb = jnp.zeros((GROUP, GROUP), BF16)
    lane_lo = lax.broadcasted_iota(jnp.int32, (CHUNK, LANES), 1) < HEAD_DIM

    def expand(x):
        xb = _bf(x)
        return jnp.where(bd_mask, jnp.concatenate([xb] * GROUP_HEADS, axis=0), zeros_gb)

    def mul(x, y):
        return _dot(_bf(x), expand(y))

    def fold(full):
        cols = []
        for t in range(GROUP // LANES):
            top = full[(2 * t) * HEAD_DIM:(2 * t + 1) * HEAD_DIM, t * LANES:(t + 1) * LANES]
            bot = full[(2 * t + 1) * HEAD_DIM:(2 * t + 2) * HEAD_DIM, t * LANES:(t + 1) * LANES]
            cols.append(jnp.where(lane_lo, top, bot))
        return jnp.concatenate(cols, axis=1)

    tril = tril_ref[...]
    bd = bd_ref[...]
    rk = rk_ref[...]
    lng = lng_ref[...]
    lnb = lnb_ref[...]

    chains = {}
    for b in range(n_seq):
        lw = lw_ref[b]
        lh, ll = _split2(lw)
        c = _dot(tril, lh) + _dot(tril, ll)
        ctot = [c[ch * CHUNK + CHUNK - 1:(ch + 1) * CHUNK, :] for ch in range(n_chunks)]
        cl = jnp.concatenate([jnp.broadcast_to(ct, (CHUNK, dim)) for ct in ctot], axis=0)
        enc = jnp.exp(-c)
        ed = jnp.exp(cl - c)
        r = r_ref[b]
        kmod = k_ref[b]
        v = v_ref[b]
        bb = bb_ref[b]
        full = dict(
            at=-(kk_ref[b] * jnp.exp(c - lw)), rt=r * jnp.exp(c), bt=bb * enc, kt=kmod * enc,
            bh=bb * ed, kh=kmod * ed, v=v, bonus=_headsum(r * kmod * rk, bd) * v)
        for ch in range(n_chunks):
            rows = slice(ch * CHUNK, (ch + 1) * CHUNK)
            for g in range(n_groups):
                lanes = slice(g * GROUP, (g + 1) * GROUP)
                cd = {name: val[rows, lanes] for name, val in full.items()}
                cd["decay"] = jnp.exp(ctot[ch][:, lanes])
                chains[(ch, b, g)] = cd

    keys = sorted(chains)
    gmat = {}
    for key in keys:
        cd = chains[key]
        lhs = _bf(jnp.concatenate([cd["at"], cd["rt"]], axis=0))
        rhs = jnp.concatenate([expand(cd["bt"]), expand(cd["kt"])], axis=0)
        gmat[key] = _dot_nt(lhs, rhs)
    for key in keys:
        cd, g = chains[key], gmat[key]
        cd["a_ab"] = jnp.where(m_strict, g[0:CHUNK, 0:GROUP], zeros_c)
        cd["a_ak"] = jnp.where(m_strict, g[0:CHUNK, GROUP:], zeros_c)
        cd["a_rb"] = jnp.where(m_incl, g[CHUNK:, 0:GROUP], zeros_c)
        cd["a_rk"] = jnp.where(m_incl, g[CHUNK:, GROUP:], zeros_c)

    pw = {key: jnp.where(same8, chains[key]["a_ab"], zeros_c) for key in keys}
    t = {key: eye + pw[key] for key in keys}
    for _ in range(2):
        pw = {key: mul(pw[key], pw[key]) for key in keys}
        upd = {key: mul(pw[key], t[key]) for key in keys}
        t = {key: t[key] + upd[key] for key in keys}
    for m_off in off_masks:
        ct = {key: mul(jnp.where(m_off, chains[key]["a_ab"], zeros_c), t[key]) for key in keys}
        upd = {key: mul(t[key], ct[key]) for key in keys}
        t = {key: t[key] + upd[key] for key in keys}

    state = {(b, g): s_ref[b, g] for b in range(n_seq) for g in range(n_groups)}
    for ch in range(n_chunks):
        cks = [key for key in keys if key[0] == ch]
        x0 = {}
        for key in cks:
            cd = chains[key]
            sh, sl = _split2(state[key[1:]])
            xin = _bf(jnp.concatenate([cd["at"], cd["rt"]], axis=0))
            x0[key] = _dot_nt(xin, expand(sh)) + _dot_nt(xin, expand(sl))
        ev = {key: expand(chains[key]["v"]) for key in cks}
        rhs_u = {key: x0[key][0:CHUNK, :] + _dot(_bf(chains[key]["a_ak"]), ev[key]) for key in cks}
        u = {}
        for key in cks:
            th, tl = _split2(t[key])
            er = expand(rhs_u[key])
            u[key] = _dot(th, er) + _dot(tl, er)
        y = {}
        for key in cks:
            cd = chains[key]
            y[key] = x0[key][CHUNK:, :] + _dot(
                _bf(jnp.concatenate([cd["a_rb"], cd["a_rk"]], axis=1)),
                jnp.concatenate([expand(u[key]), ev[key]], axis=0))
        for key in cks:
            cd = chains[key]
            uv_t = jnp.concatenate([u[key], cd["v"]], axis=0).T
            full = _dot(_bf(uv_t), _bf(jnp.concatenate([cd["bh"], cd["kh"]], axis=0)))
            state[key[1:]] = state[key[1:]] * cd["decay"] + fold(full)

        inv_n = 1.0 / HEAD_DIM
        for key in cks:
            _, b, g = key
            rows = slice(ch * CHUNK, (ch + 1) * CHUNK)
            lanes = slice(g * GROUP, (g + 1) * GROUP)
            yy = y[key]
            mu = _headsum(yy, bd) * inv_n
            yc = yy - mu
            var = _headsum(yc * yc, bd) * inv_n
            yn = yc * lax.rsqrt(var + GN_EPS) * lng[:, lanes] + lnb[:, lanes]
            out_ref[b, rows, lanes] = (yn + chains[key]["bonus"]) * gate_ref[b, rows, lanes]

    for (b, g), val in state.items():
        s_ref[b, g] = val


def _tail_kernel(x_ref, yc_ref, yr_ref, p_ref, wout_ref, gmlp_ref, wup_ref, wdown_ref,
                 gple_ref, wgate_ref, wple_ref, gfin_ref, out_ref, *, conv_dim, ff_block, final_norm):
    mix = _dot(_bf(yc_ref[...]), wout_ref[0:conv_dim, :]) + _dot(_bf(yr_ref[...]), wout_ref[conv_dim:, :])
    h = x_ref[...] + mix

    hn = _bf(_rms(h, gmlp_ref[...]))
    d_ff = wup_ref.shape[1]
    acc = None
    for j in range(d_ff // ff_block):
        cols = slice(j * ff_block, (j + 1) * ff_block)
        hid = jnp.maximum(_dot(hn, wup_ref[:, cols]), 0.0)
        part = _dot(_bf(hid * hid), wdown_ref[cols, :])
        acc = part if acc is None else acc + part
    h = h + acc

    gate = jax.nn.sigmoid(_dot(_bf(_rms(h, gple_ref[...])), wgate_ref[...]))
    h = h + gate * _dot(_bf(p_ref[...]), wple_ref[...])
    if final_norm:
        h = _rms(h, gfin_ref[...])
    out_ref[...] = h


def _const_spec(shape):
    nd = len(shape)
    return pl.BlockSpec(shape, lambda *_: (0,) * nd)


def _pad_cols(w, n):
    return jnp.pad(w, ((0, 0), (0, n - w.shape[1])))


def _pad_rows(w, n):
    return jnp.pad(w, ((0, n - w.shape[0]), (0, 0)))


def _block_diag_ones(n, blk):
    idx = np.arange(n) // blk
    return jnp.asarray(idx[:, None] == idx[None, :], dtype=BF16)


def _chunk_tril(n, blk):
    t = np.arange(n)
    same = (t[:, None] // blk) == (t[None, :] // blk)
    return jnp.asarray(same & (t[None, :] <= t[:, None]), dtype=BF16)


def kernel(x, p, norm_mix_g, w_in, conv_w, shift_mu, w_lora_up, w0, a_lora_up, a0, g_lora_up, k_k, k_a, r_k, ln_x_g, ln_x_b, w_out, norm_mlp_g, w_up, w_down, norm_ple_g, w_ple_gate, w_ple_proj, norm_final_g):
    bsz, seq, d_model = x.shape
    depth = w_in.shape[0]
    n = bsz * seq
    rwkv_dim = w_lora_up.shape[2]
    conv_dim = conv_w.shape[2]
    decay_rank = w_lora_up.shape[1]
    iclr_rank = a_lora_up.shape[1]
    gate_rank = g_lora_up.shape[1]
    c3 = 3 * conv_dim
    r3 = c3 + 3 * rwkv_dim
    assert seq % TM_PREP == 0 and seq % TM_RWKV == 0 and n % TM_TAIL == 0 and TM_RWKV % CHUNK == 0
    assert rwkv_dim % GROUP == 0 and decay_rank <= LANES and iclr_rank <= LANES and gate_rank <= 2 * LANES
    rwkv_cols = 3 * rwkv_dim + 4 * LANES

    bd = _block_diag_ones(GROUP, HEAD_DIM)
    tril = _chunk_tril(TM_RWKV, CHUNK)
    row = lambda a: a.reshape(1, -1).astype(F32)

    h = x.reshape(n, d_model)
    for li in range(depth):
        wi = w_in[li]
        s0, s1, s2 = r3, r3 + decay_rank, r3 + decay_rank + iclr_rank
        win_p = jnp.concatenate([
            wi[:, :s0], _pad_cols(wi[:, s0:s1], LANES), _pad_cols(wi[:, s1:s2], LANES),
            _pad_cols(wi[:, s2:], 2 * LANES)], axis=1).astype(BF16)
        mu = shift_mu[li].reshape(1, -1)
        o = 3 * rwkv_dim
        mu_p = jnp.concatenate([
            mu[:, :o], _pad_cols(mu[:, o:o + decay_rank], LANES),
            _pad_cols(mu[:, o + decay_rank:o + decay_rank + iclr_rank], LANES),
            _pad_cols(mu[:, o + decay_rank + iclr_rank:], 2 * LANES)], axis=1)
        ww = _pad_rows(w_lora_up[li], LANES).astype(BF16)
        wa = _pad_rows(a_lora_up[li], LANES).astype(BF16)
        wg = _pad_rows(g_lora_up[li], 2 * LANES).astype(BF16)

        row_spec = lambda w: pl.BlockSpec((TM_PREP, w), lambda i: (i, 0))
        stream = jax.ShapeDtypeStruct((n, rwkv_dim), F32)
        prep_out = pl.pallas_call(
            functools.partial(_prep_kernel, tiles_per_seq=seq // TM_PREP, conv_dim=conv_dim, rwkv_dim=rwkv_dim),
            grid=(n // TM_PREP,),
            in_specs=[
                row_spec(d_model), _const_spec((1, d_model)), _const_spec(win_p.shape),
                _const_spec((conv_w.shape[1], conv_dim)), _const_spec((1, rwkv_cols)),
                _const_spec(ww.shape), _const_spec((1, rwkv_dim)), _const_spec(wa.shape),
                _const_spec((1, rwkv_dim)), _const_spec(wg.shape), _const_spec((1, rwkv_dim)),
                _const_spec((1, rwkv_dim)), _const_spec(bd.shape)],
            out_specs=[row_spec(conv_dim)] + [row_spec(rwkv_dim)] * 7,
            out_shape=[jax.ShapeDtypeStruct((n, conv_dim), F32)] + [stream] * 7,
            scratch_shapes=[pltpu.VMEM((CARRY_ROWS, conv_dim), F32), pltpu.VMEM((CARRY_ROWS, rwkv_cols), F32)],
            compiler_params=pltpu.CompilerParams(dimension_semantics=("arbitrary",), vmem_limit_bytes=VMEM_LIMIT),
            name="prep",
        )(h, row(norm_mix_g[li]), win_p, conv_w[li], mu_p, ww, row(w0[li]), wa, row(a0[li]), wg,
          row(k_k[li]), row(k_a[li]), bd)
        y_conv = prep_out[0]
        streams = [a.reshape(bsz, seq, rwkv_dim) for a in prep_out[1:]]

        seq_spec = pl.BlockSpec((bsz, TM_RWKV, rwkv_dim), lambda j: (0, j, 0))
        y_rwkv = pl.pallas_call(
            _rwkv_kernel,
            grid=(seq // TM_RWKV,),
            in_specs=[seq_spec] * 7 + [
                _const_spec(tril.shape), _const_spec(bd.shape),
                _const_spec((1, rwkv_dim)), _const_spec((1, rwkv_dim)), _const_spec((1, rwkv_dim))],
            out_specs=seq_spec,
            out_shape=jax.ShapeDtypeStruct((bsz, seq, rwkv_dim), F32),
            scratch_shapes=[pltpu.VMEM((bsz, rwkv_dim // GROUP, HEAD_DIM, GROUP), F32)],
            compiler_params=pltpu.CompilerParams(dimension_semantics=("arbitrary",), vmem_limit_bytes=VMEM_LIMIT),
            name="rwkv",
        )(*streams, tril, bd, row(r_k[li]), row(ln_x_g[li]), row(ln_x_b[li]))
        y_rwkv = y_rwkv.reshape(n, rwkv_dim)

        tail_row = lambda w: pl.BlockSpec((TM_TAIL, w), lambda i: (i, 0))
        ple_dim = p.shape[-1]
        d_ff = w_up.shape[2]
        h = pl.pallas_call(
            functools.partial(_tail_kernel, conv_dim=conv_dim, ff_block=1024, final_norm=(li == depth - 1)),
            grid=(n // TM_TAIL,),
            in_specs=[
                tail_row(d_model), tail_row(conv_dim), tail_row(rwkv_dim), tail_row(ple_dim),
                _const_spec((d_model, d_model)), _const_spec((1, d_model)), _const_spec((d_model, d_ff)),
                _const_spec((d_ff, d_model)), _const_spec((1, d_model)), _const_spec((d_model, d_model)),
                _const_spec((ple_dim, d_model)), _const_spec((1, d_model))],
            out_specs=tail_row(d_model),
            out_shape=jax.ShapeDtypeStruct((n, d_model), F32),
            compiler_params=pltpu.CompilerParams(dimension_semantics=("parallel",), vmem_limit_bytes=VMEM_LIMIT),
            name="tail",
        )(h, y_conv, y_rwkv, p[li].reshape(n, ple_dim), w_out[li].astype(BF16), row(norm_mlp_g[li]),
          w_up[li].astype(BF16), w_down[li].astype(BF16), row(norm_ple_g[li]),
          w_ple_gate[li].astype(BF16), w_ple_proj[li].astype(BF16), row(norm_final_g))
    return h.reshape(bsz, seq, d_model)
```

```python
import functools
import math

import numpy as np
import jax
import jax.numpy as jnp
from jax import lax
from jax.experimental import pallas as pl
from jax.experimental.pallas import tpu as pltpu

F32 = jnp.float32
BF16 = jnp.bfloat16

HEAD_DIM = 64
CHUNK = 64
LANES = 128
GROUP_HEADS = 4
GROUP = GROUP_HEADS * HEAD_DIM
RMS_EPS = 1e-6
GN_EPS = 64e-5
L2_EPS = 1e-12
DECAY_SCALE = math.exp(-0.5)
CARRY_ROWS = 8

TM_PREP = 256
TM_RWKV = 256
TM_TAIL = 256
VMEM_LIMIT = 56 * 1024 * 1024


def _bf(x):
    return x.astype(BF16)


def _split2(x):
    hi = x.astype(BF16)
    lo = (x - hi.astype(F32)).astype(BF16)
    return hi, lo


def _dot(a, b):
    return jnp.dot(a, b, preferred_element_type=F32)


def _dot_nt(a, b):
    return lax.dot_general(a, b, (((1,), (1,)), ((), ())), preferred_element_type=F32)


def _headsum(x, bd):
    parts = []
    for g in range(x.shape[1] // GROUP):
        xh, xl = _split2(x[:, g * GROUP:(g + 1) * GROUP])
        parts.append(_dot(xh, bd) + _dot(xl, bd))
    return parts[0] if len(parts) == 1 else jnp.concatenate(parts, axis=1)


def _rms(x, g):
    return x * lax.rsqrt(jnp.mean(x * x, axis=-1, keepdims=True) + RMS_EPS) * g


def _shift_rows(x, carry, n):
    rows = lax.broadcasted_iota(jnp.int32, x.shape, 0)
    out = pltpu.roll(x, n, 0)
    for q in range(n):
        src = CARRY_ROWS - n + q
        out = jnp.where(rows == q, carry[src:src + 1, :], out)
    return out


def _prep_kernel(x_ref, g_ref, win_ref, convw_ref, mu_ref, ww_ref, w0_ref, wa_ref,
                 a0_ref, wg_ref, kkw_ref, ka_ref, bd_ref,
                 yconv_ref, r_ref, k_ref, v_ref, lw_ref, kk_ref, bb_ref, gate_ref,
                 ucar_ref, ccar_ref, *, tiles_per_seq, conv_dim, rwkv_dim):
    i = pl.program_id(0)

    @pl.when(i % tiles_per_seq == 0)
    def _():
        ucar_ref[...] = jnp.zeros_like(ucar_ref)
        ccar_ref[...] = jnp.zeros_like(ccar_ref)

    tm = x_ref.shape[0]
    xb = _bf(_rms(x_ref[...], g_ref[...]))

    c3 = 3 * conv_dim
    pa = _dot(xb, win_ref[:, 0:c3])
    gate_b = pa[:, 0:conv_dim]
    u = pa[:, conv_dim:2 * conv_dim] * pa[:, 2 * conv_dim:c3]
    ucar = ucar_ref[...]
    cw = convw_ref[...]
    conv = (_shift_rows(u, ucar, 2) * cw[0:1, :] + _shift_rows(u, ucar, 1) * cw[1:2, :]
            + u * cw[2:3, :])
    yconv_ref[...] = gate_b * conv
    ucar_ref[...] = u[tm - CARRY_ROWS:tm, :]

    pb = _dot(xb, win_ref[:, c3:])
    prev = _shift_rows(pb, ccar_ref[...], 1)
    uu = pb + mu_ref[...] * (prev - pb)
    ccar_ref[...] = pb[tm - CARRY_ROWS:tm, :]

    d = rwkv_dim
    r = uu[:, 0:d]
    k = uu[:, d:2 * d]
    v = uu[:, 2 * d:3 * d]
    xw = uu[:, 3 * d:3 * d + LANES]
    xa = uu[:, 3 * d + LANES:3 * d + 2 * LANES]
    xg = uu[:, 3 * d + 2 * LANES:]

    z = w0_ref[...] + _dot(_bf(jnp.tanh(xw)), ww_ref[...])
    lw = -DECAY_SCALE * jax.nn.sigmoid(z)
    iclr = jax.nn.sigmoid(a0_ref[...] + _dot(_bf(xa), wa_ref[...]))
    gate = _dot(_bf(jax.nn.sigmoid(xg)), wg_ref[...])

    kk = k * kkw_ref[...]
    nrm = jnp.sqrt(_headsum(kk * kk, bd_ref[...]))
    kk = kk / jnp.maximum(nrm, L2_EPS)

    r_ref[...] = r
    k_ref[...] = k * (1.0 + (iclr - 1.0) * ka_ref[...])
    v_ref[...] = v
    lw_ref[...] = lw
    kk_ref[...] = kk
    bb_ref[...] = kk * iclr
    gate_ref[...] = gate


def _rwkv_kernel(r_ref, k_ref, v_ref, lw_ref, kk_ref, bb_ref, gate_ref,
                 tril_ref, bd_ref, rk_ref, lng_ref, lnb_ref,
                 out_ref, s_ref):
    @pl.when(pl.program_id(0) == 0)
    def _():
        s_ref[...] = jnp.zeros_like(s_ref)

    n_seq, tm, dim = r_ref.shape
    n_groups = dim // GROUP
    n_chunks = tm // CHUNK

    ri = lax.broadcasted_iota(jnp.int32, (CHUNK, GROUP), 0)
    ji = lax.broadcasted_iota(jnp.int32, (CHUNK, GROUP), 1) % HEAD_DIM
    m_strict = ji < ri
    m_incl = ji <= ri
    eye = (ji == ri).astype(F32)
    same8 = (ji // 8) == (ri // 8)
    off_masks = [((ji // (2 * s)) == (ri // (2 * s))) & ((ji // s) != (ri // s)) for s in (8, 16, 32)]
    zeros_c = jnp.zeros((CHUNK, GROUP), F32)
    gr = lax.broadcasted_iota(jnp.int32, (GROUP, GROUP), 0) // HEAD_DIM
    gc = lax.broadcasted_iota(jnp.int32, (GROUP, GROUP), 1) // HEAD_DIM
    bd_mask = gr == gc
    zeros_gb = jnp.zeros((GROUP, GROUP), BF16)
    lane_lo = lax.broadcasted_iota(jnp.int32, (CHUNK, LANES), 1) < HEAD_DIM

    def expand(x):
        xb = _bf(x)
        return jnp.where(bd_mask, jnp.concatenate([xb] * GROUP_HEADS, axis=0), zeros_gb)

    def mul(x, y):
        return _dot(_bf(x), expand(y))

    def fold(full):
        cols = []
        for t in range(GROUP // LANES):
            top = full[(2 * t) * HEAD_DIM:(2 * t + 1) * HEAD_DIM, t * LANES:(t + 1) * LANES]
            bot = full[(2 * t + 1) * HEAD_DIM:(2 * t + 2) * HEAD_DIM, t * LANES:(t + 1) * LANES]
            cols.append(jnp.where(lane_lo, top, bot))
        return jnp.concatenate(cols, axis=1)

    tril = tril_ref[...]
    bd = bd_ref[...]
    rk = rk_ref[...]
    lng = lng_ref[...]
    lnb = lnb_ref[...]

    chains = {}
    for b in range(n_seq):
        lw = lw_ref[b]
        lh, ll = _split2(lw)
        c = _dot(tril, lh) + _dot(tril, ll)
        ctot = [c[ch * CHUNK + CHUNK - 1:(ch + 1) * CHUNK, :] for ch in range(n_chunks)]
        cl = jnp.concatenate([jnp.broadcast_to(ct, (CHUNK, dim)) for ct in ctot], axis=0)
        enc = jnp.exp(-c)
        ed = jnp.exp(cl - c)
        r = r_ref[b]
        kmod = k_ref[b]
        v = v_ref[b]
        bb = bb_ref[b]
        full = dict(
            at=-(kk_ref[b] * jnp.exp(c - lw)), rt=r * jnp.exp(c), bt=bb * enc, kt=kmod * enc,
            bh=bb * ed, kh=kmod * ed, v=v, bonus=_headsum(r * kmod * rk, bd) * v)
        for ch in range(n_chunks):
            rows = slice(ch * CHUNK, (ch + 1) * CHUNK)
            for g in range(n_groups):
                lanes = slice(g * GROUP, (g + 1) * GROUP)
                cd = {name: val[rows, lanes] for name, val in full.items()}
                cd["decay"] = jnp.exp(ctot[ch][:, lanes])
                chains[(ch, b, g)] = cd

    keys = sorted(chains)
    gmat = {}
    for key in keys:
        cd = chains[key]
        lhs = _bf(jnp.concatenate([cd["at"], cd["rt"]], axis=0))
        rhs = jnp.concatenate([expand(cd["bt"]), expand(cd["kt"])], axis=0)
        gmat[key] = _dot_nt(lhs, rhs)
    for key in keys:
        cd, g = chains[key], gmat[key]
        cd["a_ab"] = jnp.where(m_strict, g[0:CHUNK, 0:GROUP], zeros_c)
        cd["a_ak"] = jnp.where(m_strict, g[0:CHUNK, GROUP:], zeros_c)
        cd["a_rb"] = jnp.where(m_incl, g[CHUNK:, 0:GROUP], zeros_c)
        cd["a_rk"] = jnp.where(m_incl, g[CHUNK:, GROUP:], zeros_c)

    pw = {key: jnp.where(same8, chains[key]["a_ab"], zeros_c) for key in keys}
    t = {key: eye + pw[key] for key in keys}
    for _ in range(2):
        pw = {key: mul(pw[key], pw[key]) for key in keys}
        upd = {key: mul(pw[key], t[key]) for key in keys}
        t = {key: t[key] + upd[key] for key in keys}
    for m_off in off_masks:
        ct = {key: mul(jnp.where(m_off, chains[key]["a_ab"], zeros_c), t[key]) for key in keys}
        upd = {key: mul(t[key], ct[key]) for key in keys}
        t = {key: t[key] + upd[key] for key in keys}

    state = {(b, g): s_ref[b, g] for b in range(n_seq) for g in range(n_groups)}
    inv_n = 1.0 / HEAD_DIM

    def chunk_keys(ch):
        return [key for key in keys if key[0] == ch]

    def norm_stage(ch, stage, carry):
        cks = chunk_keys(ch)
        if stage == 0:
            yy = jnp.concatenate([carry[key] for key in cks], axis=0)
            return yy, _headsum(yy, bd) * inv_n
        if stage == 1:
            yy, mu = carry
            yc = yy - mu
            return yc, _headsum(yc * yc, bd) * inv_n
        yc, var = carry
        yn = yc * lax.rsqrt(var + GN_EPS)
        for idx, (_, b, g) in enumerate(cks):
            rows = slice(ch * CHUNK, (ch + 1) * CHUNK)
            lanes = slice(g * GROUP, (g + 1) * GROUP)
            part = yn[idx * CHUNK:(idx + 1) * CHUNK, :] * lng[:, lanes] + lnb[:, lanes]
            out_ref[b, rows, lanes] = (part + chains[(ch, b, g)]["bonus"]) * gate_ref[b, rows, lanes]
        return None

    pending = None
    for ch in range(n_chunks + 1):
        cks = chunk_keys(ch) if ch < n_chunks else []
        x0 = {}
        for key in cks:
            cd = chains[key]
            xin = _bf(jnp.concatenate([cd["at"], cd["rt"]], axis=0))
            x0[key] = _dot_nt(xin, expand(state[key[1:]]))
        ev = {key: expand(chains[key]["v"]) for key in cks}
        rhs_u = {key: x0[key][0:CHUNK, :] + _dot(_bf(chains[key]["a_ak"]), ev[key]) for key in cks}
        if ch > 0:
            pending = norm_stage(ch - 1, 0, pending)
        u = {key: mul(t[key], rhs_u[key]) for key in cks}
        if ch > 0:
            pending = norm_stage(ch - 1, 1, pending)
        y = {}
        for key in cks:
            cd = chains[key]
            y[key] = x0[key][CHUNK:, :] + _dot(
                _bf(jnp.concatenate([cd["a_rb"], cd["a_rk"]], axis=1)),
                jnp.concatenate([expand(u[key]), ev[key]], axis=0))
        for key in cks:
            cd = chains[key]
            uv_t = jnp.concatenate([u[key], cd["v"]], axis=0).T
            full = _dot(_bf(uv_t), _bf(jnp.concatenate([cd["bh"], cd["kh"]], axis=0)))
            state[key[1:]] = state[key[1:]] * cd["decay"] + fold(full)
        if ch > 0:
            norm_stage(ch - 1, 2, pending)
        pending = y

    for (b, g), val in state.items():
        s_ref[b, g] = val


def _tail_kernel(x_ref, yc_ref, yr_ref, p_ref, wout_ref, gmlp_ref, wup_ref, wdown_ref,
                 gple_ref, wgate_ref, wple_ref, gfin_ref, out_ref, *, conv_dim, ff_block, final_norm):
    mix = _dot(_bf(yc_ref[...]), wout_ref[0:conv_dim, :]) + _dot(_bf(yr_ref[...]), wout_ref[conv_dim:, :])
    h = x_ref[...] + mix

    hn = _bf(_rms(h, gmlp_ref[...]))
    d_ff = wup_ref.shape[1]
    acc = None
    for j in range(d_ff // ff_block):
        cols = slice(j * ff_block, (j + 1) * ff_block)
        hid = jnp.maximum(_dot(hn, wup_ref[:, cols]), 0.0)
        part = _dot(_bf(hid * hid), wdown_ref[cols, :])
        acc = part if acc is None else acc + part
    h = h + acc

    gate = jax.nn.sigmoid(_dot(_bf(_rms(h, gple_ref[...])), wgate_ref[...]))
    h = h + gate * _dot(_bf(p_ref[...]), wple_ref[...])
    if final_norm:
        h = _rms(h, gfin_ref[...])
    out_ref[...] = h


def _const_spec(shape):
    nd = len(shape)
    return pl.BlockSpec(shape, lambda *_: (0,) * nd)


def _pad_cols(w, n):
    return jnp.pad(w, ((0, 0), (0, n - w.shape[1])))


def _pad_rows(w, n):
    return jnp.pad(w, ((0, n - w.shape[0]), (0, 0)))


def _block_diag_ones(n, blk):
    idx = np.arange(n) // blk
    return jnp.asarray(idx[:, None] == idx[None, :], dtype=BF16)


def _chunk_tril(n, blk):
    t = np.arange(n)
    same = (t[:, None] // blk) == (t[None, :] // blk)
    return jnp.asarray(same & (t[None, :] <= t[:, None]), dtype=BF16)


def kernel(x, p, norm_mix_g, w_in, conv_w, shift_mu, w_lora_up, w0, a_lora_up, a0, g_lora_up, k_k, k_a, r_k, ln_x_g, ln_x_b, w_out, norm_mlp_g, w_up, w_down, norm_ple_g, w_ple_gate, w_ple_proj, norm_final_g):
    bsz, seq, d_model = x.shape
    depth = w_in.shape[0]
    n = bsz * seq
    rwkv_dim = w_lora_up.shape[2]
    conv_dim = conv_w.shape[2]
    decay_rank = w_lora_up.shape[1]
    iclr_rank = a_lora_up.shape[1]
    gate_rank = g_lora_up.shape[1]
    c3 = 3 * conv_dim
    r3 = c3 + 3 * rwkv_dim
    assert seq % TM_PREP == 0 and seq % TM_RWKV == 0 and n % TM_TAIL == 0 and TM_RWKV % CHUNK == 0
    assert rwkv_dim % GROUP == 0 and decay_rank <= LANES and iclr_rank <= LANES and gate_rank <= 2 * LANES
    rwkv_cols = 3 * rwkv_dim + 4 * LANES

    bd = _block_diag_ones(GROUP, HEAD_DIM)
    tril = _chunk_tril(TM_RWKV, CHUNK)
    row = lambda a: a.reshape(1, -1).astype(F32)

    h = x.reshape(n, d_model)
    for li in range(depth):
        wi = w_in[li]
        s0, s1, s2 = r3, r3 + decay_rank, r3 + decay_rank + iclr_rank
        win_p = jnp.concatenate([
            wi[:, :s0], _pad_cols(wi[:, s0:s1], LANES), _pad_cols(wi[:, s1:s2], LANES),
            _pad_cols(wi[:, s2:], 2 * LANES)], axis=1).astype(BF16)
        mu = shift_mu[li].reshape(1, -1)
        o = 3 * rwkv_dim
        mu_p = jnp.concatenate([
            mu[:, :o], _pad_cols(mu[:, o:o + decay_rank], LANES),
            _pad_cols(mu[:, o + decay_rank:o + decay_rank + iclr_rank], LANES),
            _pad_cols(mu[:, o + decay_rank + iclr_rank:], 2 * LANES)], axis=1)
        ww = _pad_rows(w_lora_up[li], LANES).astype(BF16)
        wa = _pad_rows(a_lora_up[li], LANES).astype(BF16)
        wg = _pad_rows(g_lora_up[li], 2 * LANES).astype(BF16)

        row_spec = lambda w: pl.BlockSpec((TM_PREP, w), lambda i: (i, 0))
        stream = jax.ShapeDtypeStruct((n, rwkv_dim), F32)
        prep_out = pl.pallas_call(
            functools.partial(_prep_kernel, tiles_per_seq=seq // TM_PREP, conv_dim=conv_dim, rwkv_dim=rwkv_dim),
            grid=(n // TM_PREP,),
            in_specs=[
                row_spec(d_model), _const_spec((1, d_model)), _const_spec(win_p.shape),
                _const_spec((conv_w.shape[1], conv_dim)), _const_spec((1, rwkv_cols)),
                _const_spec(ww.shape), _const_spec((1, rwkv_dim)), _const_spec(wa.shape),
                _const_spec((1, rwkv_dim)), _const_spec(wg.shape), _const_spec((1, rwkv_dim)),
                _const_spec((1, rwkv_dim)), _const_spec(bd.shape)],
            out_specs=[row_spec(conv_dim)] + [row_spec(rwkv_dim)] * 7,
            out_shape=[jax.ShapeDtypeStruct((n, conv_dim), F32)] + [stream] * 7,
            scratch_shapes=[pltpu.VMEM((CARRY_ROWS, conv_dim), F32), pltpu.VMEM((CARRY_ROWS, rwkv_cols), F32)],
            compiler_params=pltpu.CompilerParams(dimension_semantics=("arbitrary",), vmem_limit_bytes=VMEM_LIMIT),
            name="prep",
        )(h, row(norm_mix_g[li]), win_p, conv_w[li], mu_p, ww, row(w0[li]), wa, row(a0[li]), wg,
          row(k_k[li]), row(k_a[li]), bd)
        y_conv = prep_out[0]
        streams = [a.reshape(bsz, seq, rwkv_dim) for a in prep_out[1:]]

        seq_spec = pl.BlockSpec((bsz, TM_RWKV, rwkv_dim), lambda j: (0, j, 0))
        y_rwkv = pl.pallas_call(
            _rwkv_kernel,
            grid=(seq // TM_RWKV,),
            in_specs=[seq_spec] * 7 + [
                _const_spec(tril.shape), _const_spec(bd.shape),
                _const_spec((1, rwkv_dim)), _const_spec((1, rwkv_dim)), _const_spec((1, rwkv_dim))],
            out_specs=seq_spec,
            out_shape=jax.ShapeDtypeStruct((bsz, seq, rwkv_dim), F32),
            scratch_shapes=[pltpu.VMEM((bsz, rwkv_dim // GROUP, HEAD_DIM, GROUP), F32)],
            compiler_params=pltpu.CompilerParams(dimension_semantics=("arbitrary",), vmem_limit_bytes=VMEM_LIMIT),
            name="rwkv",
        )(*streams, tril, bd, row(r_k[li]), row(ln_x_g[li]), row(ln_x_b[li]))
        y_rwkv = y_rwkv.reshape(n, rwkv_dim)

        tail_row = lambda w: pl.BlockSpec((TM_TAIL, w), lambda i: (i, 0))
        ple_dim = p.shape[-1]
        d_ff = w_up.shape[2]
        h = pl.pallas_call(
            functools.partial(_tail_kernel, conv_dim=conv_dim, ff_block=1024, final_norm=(li == depth - 1)),
            grid=(n // TM_TAIL,),
            in_specs=[
                tail_row(d_model), tail_row(conv_dim), tail_row(rwkv_dim), tail_row(ple_dim),
                _const_spec((d_model, d_model)), _const_spec((1, d_model)), _const_spec((d_model, d_ff)),
                _const_spec((d_ff, d_model)), _const_spec((1, d_model)), _const_spec((d_model, d_model)),
                _const_spec((ple_dim, d_model)), _const_spec((1, d_model))],
            out_specs=tail_row(d_model),
            out_shape=jax.ShapeDtypeStruct((n, d_model), F32),
            compiler_params=pltpu.CompilerParams(dimension_semantics=("parallel",), vmem_limit_bytes=VMEM_LIMIT),
            name="tail",
        )(h, y_conv, y_rwkv, p[li].reshape(n, ple_dim), w_out[li].astype(BF16), row(norm_mlp_g[li]),
          w_up[li].astype(BF16), w_down[li].astype(BF16), row(norm_ple_g[li]),
          w_ple_gate[li].astype(BF16), w_ple_proj[li].astype(BF16), row(norm_final_g))
    return h.reshape(bsz, seq, d_model)
```

```python
import functools
import math

import numpy as np
import jax
import jax.numpy as jnp
from jax import lax
from jax.experimental import pallas as pl
from jax.experimental.pallas import tpu as pltpu

F32 = jnp.float32
BF16 = jnp.bfloat16

HEAD_DIM = 64
CHUNK = 64
LANES = 128
GROUP_HEADS = 4
GROUP = GROUP_HEADS * HEAD_DIM
RMS_EPS = 1e-6
GN_EPS = 64e-5
L2_EPS = 1e-12
DECAY_SCALE = math.exp(-0.5)
CARRY_ROWS = 8

TM_PREP = 512
PREP_PARTS = 2
TM_RWKV = 256
TM_TAIL = 512
TAIL_PARTS = 2
VMEM_LIMIT = 56 * 1024 * 1024


def _bf(x):
    return x.astype(BF16)


def _split2(x):
    hi = x.astype(BF16)
    lo = (x - hi.astype(F32)).astype(BF16)
    return hi, lo


def _dot(a, b):
    return jnp.dot(a, b, preferred_element_type=F32)


def _dot_nt(a, b):
    return lax.dot_general(a, b, (((1,), (1,)), ((), ())), preferred_element_type=F32)


def _headsum(x, bd):
    parts = []
    for g in range(x.shape[1] // GROUP):
        xh, xl = _split2(x[:, g * GROUP:(g + 1) * GROUP])
        parts.append(_dot(xh, bd) + _dot(xl, bd))
    return parts[0] if len(parts) == 1 else jnp.concatenate(parts, axis=1)


def _rms(x, g):
    return x * lax.rsqrt(jnp.mean(x * x, axis=-1, keepdims=True) + RMS_EPS) * g


def _shift_rows(x, carry, n):
    rows = lax.broadcasted_iota(jnp.int32, x.shape, 0)
    out = pltpu.roll(x, n, 0)
    for q in range(n):
        src = CARRY_ROWS - n + q
        out = jnp.where(rows == q, carry[src:src + 1, :], out)
    return out


def _prep_kernel(x_ref, g_ref, win_ref, convw_ref, mu_ref, ww_ref, w0_ref, wa_ref,
                 a0_ref, wg_ref, kkw_ref, ka_ref, bd_ref,
                 yconv_ref, r_ref, k_ref, v_ref, lw_ref, kk_ref, bb_ref, gate_ref,
                 ucar_ref, ccar_ref, *, tiles_per_seq, conv_dim, rwkv_dim):
    i = pl.program_id(0)

    @pl.when(i % tiles_per_seq == 0)
    def _():
        ucar_ref[...] = jnp.zeros_like(ucar_ref)
        ccar_ref[...] = jnp.zeros_like(ccar_ref)

    tm = x_ref.shape[0]
    pr = tm // PREP_PARTS
    parts = [slice(q * pr, (q + 1) * pr) for q in range(PREP_PARTS)]
    c3 = 3 * conv_dim
    d = rwkv_dim
    cw = convw_ref[...]

    xb = [_bf(_rms(x_ref[rs, :], g_ref[...])) for rs in parts]
    pa = [_dot(xq, win_ref[:, 0:c3]) for xq in xb]
    pb = [_dot(xq, win_ref[:, c3:]) for xq in xb]

    ucar = ucar_ref[...]
    for q, rs in enumerate(parts):
        gate_b = pa[q][:, 0:conv_dim]
        u = pa[q][:, conv_dim:2 * conv_dim] * pa[q][:, 2 * conv_dim:c3]
        conv = (_shift_rows(u, ucar, 2) * cw[0:1, :] + _shift_rows(u, ucar, 1) * cw[1:2, :]
                + u * cw[2:3, :])
        yconv_ref[rs, :] = gate_b * conv
        ucar = u[pr - CARRY_ROWS:pr, :]
    ucar_ref[...] = ucar

    ccar = ccar_ref[...]
    uu = []
    for q in range(PREP_PARTS):
        prev = _shift_rows(pb[q], ccar, 1)
        uu.append(pb[q] + mu_ref[...] * (prev - pb[q]))
        ccar = pb[q][pr - CARRY_ROWS:pr, :]
    ccar_ref[...] = ccar

    z = [w0_ref[...] + _dot(_bf(jnp.tanh(uq[:, 3 * d:3 * d + LANES])), ww_ref[...]) for uq in uu]
    za = [a0_ref[...] + _dot(_bf(uq[:, 3 * d + LANES:3 * d + 2 * LANES]), wa_ref[...]) for uq in uu]
    gate = [_dot(_bf(jax.nn.sigmoid(uq[:, 3 * d + 2 * LANES:])), wg_ref[...]) for uq in uu]
    kk = [uq[:, d:2 * d] * kkw_ref[...] for uq in uu]
    ssq = [_headsum(kq * kq, bd_ref[...]) for kq in kk]

    for q, rs in enumerate(parts):
        iclr = jax.nn.sigmoid(za[q])
        kn = kk[q] / jnp.maximum(jnp.sqrt(ssq[q]), L2_EPS)
        r_ref[rs, :] = uu[q][:, 0:d]
        k_ref[rs, :] = uu[q][:, d:2 * d] * (1.0 + (iclr - 1.0) * ka_ref[...])
        v_ref[rs, :] = uu[q][:, 2 * d:3 * d]
        lw_ref[rs, :] = -DECAY_SCALE * jax.nn.sigmoid(z[q])
        kk_ref[rs, :] = kn
        bb_ref[rs, :] = kn * iclr
        gate_ref[rs, :] = gate[q]


def _rwkv_kernel(r_ref, k_ref, v_ref, lw_ref, kk_ref, bb_ref, gate_ref,
                 tril_ref, bd_ref, rk_ref, lng_ref, lnb_ref,
                 out_ref, s_ref):
    @pl.when(pl.program_id(0) == 0)
    def _():
        s_ref[...] = jnp.zeros_like(s_ref)

    n_seq, tm, dim = r_ref.shape
    n_groups = dim // GROUP
    n_chunks = tm // CHUNK

    ri = lax.broadcasted_iota(jnp.int32, (CHUNK, GROUP), 0)
    ji = lax.broadcasted_iota(jnp.int32, (CHUNK, GROUP), 1) % HEAD_DIM
    m_strict = ji < ri
    m_incl = ji <= ri
    eye = (ji == ri).astype(F32)
    same8 = (ji // 8) == (ri // 8)
    off_masks = [((ji // (2 * s)) == (ri // (2 * s))) & ((ji // s) != (ri // s)) for s in (8, 16, 32)]
    zeros_c = jnp.zeros((CHUNK, GROUP), F32)
    gr = lax.broadcasted_iota(jnp.int32, (GROUP, GROUP), 0) // HEAD_DIM
    gc = lax.broadcasted_iota(jnp.int32, (GROUP, GROUP), 1) // HEAD_DIM
    bd_mask = gr == gc
    zeros_gb = jnp.zeros((GROUP, GROUP), BF16)
    lane_lo = lax.broadcasted_iota(jnp.int32, (CHUNK, LANES), 1) < HEAD_DIM

    def expand(x):
        xb = _bf(x)
        return jnp.where(bd_mask, jnp.concatenate([xb] * GROUP_HEADS, axis=0), zeros_gb)

    def mul(x, y):
        return _dot(_bf(x), expand(y))

    def fold(full):
        cols = []
        for t in range(GROUP // LANES):
            top = full[(2 * t) * HEAD_DIM:(2 * t + 1) * HEAD_DIM, t * LANES:(t + 1) * LANES]
            bot = full[(2 * t + 1) * HEAD_DIM:(2 * t + 2) * HEAD_DIM, t * LANES:(t + 1) * LANES]
            cols.append(jnp.where(lane_lo, top, bot))
        return jnp.concatenate(cols, axis=1)

    tril = tril_ref[...]
    bd = bd_ref[...]
    rk = rk_ref[...]
    lng = lng_ref[...]
    lnb = lnb_ref[...]

    chains = {}
    for b in range(n_seq):
        lw = lw_ref[b]
        lh, ll = _split2(lw)
        c = _dot(tril, lh) + _dot(tril, ll)
        ctot = [c[ch * CHUNK + CHUNK - 1:(ch + 1) * CHUNK, :] for ch in range(n_chunks)]
        cl = jnp.concatenate([jnp.broadcast_to(ct, (CHUNK, dim)) for ct in ctot], axis=0)
        enc = jnp.exp(-c)
        ed = jnp.exp(cl - c)
        r = r_ref[b]
        kmod = k_ref[b]
        v = v_ref[b]
        bb = bb_ref[b]
        full = dict(
            at=-(kk_ref[b] * jnp.exp(c - lw)), rt=r * jnp.exp(c), bt=bb * enc, kt=kmod * enc,
            bh=bb * ed, kh=kmod * ed, v=v, bonus=_headsum(r * kmod * rk, bd) * v)
        for ch in range(n_chunks):
            rows = slice(ch * CHUNK, (ch + 1) * CHUNK)
            for g in range(n_groups):
                lanes = slice(g * GROUP, (g + 1) * GROUP)
                cd = {name: val[rows, lanes] for name, val in full.items()}
                cd["decay"] = jnp.exp(ctot[ch][:, lanes])
                chains[(ch, b, g)] = cd

    keys = sorted(chains)
    gmat = {}
    for key in keys:
        cd = chains[key]
        lhs = _bf(jnp.concatenate([cd["at"], cd["rt"]], axis=0))
        rhs = jnp.concatenate([expand(cd["bt"]), expand(cd["kt"])], axis=0)
        gmat[key] = _dot_nt(lhs, rhs)
    for key in keys:
        cd, g = chains[key], gmat[key]
        cd["a_ab"] = jnp.where(m_strict, g[0:CHUNK, 0:GROUP], zeros_c)
        cd["a_ak"] = jnp.where(m_strict, g[0:CHUNK, GROUP:], zeros_c)
        cd["a_rb"] = jnp.where(m_incl, g[CHUNK:, 0:GROUP], zeros_c)
        cd["a_rk"] = jnp.where(m_incl, g[CHUNK:, GROUP:], zeros_c)

    pw = {key: jnp.where(same8, chains[key]["a_ab"], zeros_c) for key in keys}
    t = {key: eye + pw[key] for key in keys}
    for _ in range(2):
        pw = {key: mul(pw[key], pw[key]) for key in keys}
        upd = {key: mul(pw[key], t[key]) for key in keys}
        t = {key: t[key] + upd[key] for key in keys}
    for m_off in off_masks:
        ct = {key: mul(jnp.where(m_off, chains[key]["a_ab"], zeros_c), t[key]) for key in keys}
        upd = {key: mul(t[key], ct[key]) for key in keys}
        t = {key: t[key] + upd[key] for key in keys}

    state = {(b, g): s_ref[b, g] for b in range(n_seq) for g in range(n_groups)}
    inv_n = 1.0 / HEAD_DIM

    def chunk_keys(ch):
        return [key for key in keys if key[0] == ch]

    def norm_stage(ch, stage, carry):
        cks = chunk_keys(ch)
        if stage == 0:
            yy = jnp.concatenate([carry[key] for key in cks], axis=0)
            return yy, _headsum(yy, bd) * inv_n
        if stage == 1:
            yy, mu = carry
            yc = yy - mu
            return yc, _headsum(yc * yc, bd) * inv_n
        yc, var = carry
        yn = yc * lax.rsqrt(var + GN_EPS)
        for idx, (_, b, g) in enumerate(cks):
            rows = slice(ch * CHUNK, (ch + 1) * CHUNK)
            lanes = slice(g * GROUP, (g + 1) * GROUP)
            part = yn[idx * CHUNK:(idx + 1) * CHUNK, :] * lng[:, lanes] + lnb[:, lanes]
            out_ref[b, rows, lanes] = (part + chains[(ch, b, g)]["bonus"]) * gate_ref[b, rows, lanes]
        return None

    pending = None
    for ch in range(n_chunks + 1):
        cks = chunk_keys(ch) if ch < n_chunks else []
        x0 = {}
        for key in cks:
            cd = chains[key]
            xin = _bf(jnp.concatenate([cd["at"], cd["rt"]], axis=0))
            x0[key] = _dot_nt(xin, expand(state[key[1:]]))
        ev = {key: expand(chains[key]["v"]) for key in cks}
        rhs_u = {key: x0[key][0:CHUNK, :] + _dot(_bf(chains[key]["a_ak"]), ev[key]) for key in cks}
        if ch > 0:
            pending = norm_stage(ch - 1, 0, pending)
        u = {key: mul(t[key], rhs_u[key]) for key in cks}
        if ch > 0:
            pending = norm_stage(ch - 1, 1, pending)
        y = {}
        for key in cks:
            cd = chains[key]
            y[key] = x0[key][CHUNK:, :] + _dot(
                _bf(jnp.concatenate([cd["a_rb"], cd["a_rk"]], axis=1)),
                jnp.concatenate([expand(u[key]), ev[key]], axis=0))
        for key in cks:
            cd = chains[key]
            uv_t = jnp.concatenate([u[key], cd["v"]], axis=0).T
            full = _dot(_bf(uv_t), _bf(jnp.concatenate([cd["bh"], cd["kh"]], axis=0)))
            state[key[1:]] = state[key[1:]] * cd["decay"] + fold(full)
        if ch > 0:
            norm_stage(ch - 1, 2, pending)
        pending = y

    for (b, g), val in state.items():
        s_ref[b, g] = val


def _tail_kernel(x_ref, yc_ref, yr_ref, p_ref, wout_ref, gmlp_ref, wup_ref, wdown_ref,
                 gple_ref, wgate_ref, wple_ref, gfin_ref, out_ref, *, conv_dim, ff_block, final_norm):
    tm = x_ref.shape[0]
    parts = [slice(q * (tm // TAIL_PARTS), (q + 1) * (tm // TAIL_PARTS)) for q in range(TAIL_PARTS)]
    mix = [_dot(_bf(yc_ref[rs, :]), wout_ref[0:conv_dim, :]) + _dot(_bf(yr_ref[rs, :]), wout_ref[conv_dim:, :])
           for rs in parts]
    h = [x_ref[rs, :] + m for rs, m in zip(parts, mix)]

    hn = [_bf(_rms(hq, gmlp_ref[...])) for hq in h]
    d_ff = wup_ref.shape[1]
    acc = [None] * TAIL_PARTS
    for j in range(d_ff // ff_block):
        cols = slice(j * ff_block, (j + 1) * ff_block)
        for q in range(TAIL_PARTS):
            hid = jnp.maximum(_dot(hn[q], wup_ref[:, cols]), 0.0)
            part = _dot(_bf(hid * hid), wdown_ref[cols, :])
            acc[q] = part if acc[q] is None else acc[q] + part
    h = [hq + aq for hq, aq in zip(h, acc)]

    gate = [jax.nn.sigmoid(_dot(_bf(_rms(hq, gple_ref[...])), wgate_ref[...])) for hq in h]
    ple = [_dot(_bf(p_ref[rs, :]), wple_ref[...]) for rs in parts]
    for q, rs in enumerate(parts):
        hq = h[q] + gate[q] * ple[q]
        if final_norm:
            hq = _rms(hq, gfin_ref[...])
        out_ref[rs, :] = hq


def _const_spec(shape):
    nd = len(shape)
    return pl.BlockSpec(shape, lambda *_: (0,) * nd, pipeline_mode=pl.Buffered(1))


def _pad_cols(w, n):
    return jnp.pad(w, ((0, 0), (0, n - w.shape[1])))


def _pad_rows(w, n):
    return jnp.pad(w, ((0, n - w.shape[0]), (0, 0)))


def _block_diag_ones(n, blk):
    idx = np.arange(n) // blk
    return jnp.asarray(idx[:, None] == idx[None, :], dtype=BF16)


def _chunk_tril(n, blk):
    t = np.arange(n)
    same = (t[:, None] // blk) == (t[None, :] // blk)
    return jnp.asarray(same & (t[None, :] <= t[:, None]), dtype=BF16)


def kernel(x, p, norm_mix_g, w_in, conv_w, shift_mu, w_lora_up, w0, a_lora_up, a0, g_lora_up, k_k, k_a, r_k, ln_x_g, ln_x_b, w_out, norm_mlp_g, w_up, w_down, norm_ple_g, w_ple_gate, w_ple_proj, norm_final_g):
    bsz, seq, d_model = x.shape
    depth = w_in.shape[0]
    n = bsz * seq
    rwkv_dim = w_lora_up.shape[2]
    conv_dim = conv_w.shape[2]
    decay_rank = w_lora_up.shape[1]
    iclr_rank = a_lora_up.shape[1]
    gate_rank = g_lora_up.shape[1]
    c3 = 3 * conv_dim
    r3 = c3 + 3 * rwkv_dim
    assert seq % TM_PREP == 0 and seq % TM_RWKV == 0 and n % TM_TAIL == 0 and TM_RWKV % CHUNK == 0
    assert rwkv_dim % GROUP == 0 and decay_rank <= LANES and iclr_rank <= LANES and gate_rank <= 2 * LANES
    rwkv_cols = 3 * rwkv_dim + 4 * LANES

    bd = _block_diag_ones(GROUP, HEAD_DIM)
    tril = _chunk_tril(TM_RWKV, CHUNK)
    row = lambda a: a.reshape(1, -1).astype(F32)

    h = x.reshape(n, d_model)
    for li in range(depth):
        wi = w_in[li]
        s0, s1, s2 = r3, r3 + decay_rank, r3 + decay_rank + iclr_rank
        win_p = jnp.concatenate([
            wi[:, :s0], _pad_cols(wi[:, s0:s1], LANES), _pad_cols(wi[:, s1:s2], LANES),
            _pad_cols(wi[:, s2:], 2 * LANES)], axis=1).astype(BF16)
        mu = shift_mu[li].reshape(1, -1)
        o = 3 * rwkv_dim
        mu_p = jnp.concatenate([
            mu[:, :o], _pad_cols(mu[:, o:o + decay_rank], LANES),
            _pad_cols(mu[:, o + decay_rank:o + decay_rank + iclr_rank], LANES),
            _pad_cols(mu[:, o + decay_rank + iclr_rank:], 2 * LANES)], axis=1)
        ww = _pad_rows(w_lora_up[li], LANES).astype(BF16)
        wa = _pad_rows(a_lora_up[li], LANES).astype(BF16)
        wg = _pad_rows(g_lora_up[li], 2 * LANES).astype(BF16)

        row_spec = lambda w: pl.BlockSpec((TM_PREP, w), lambda i: (i, 0))
        stream = jax.ShapeDtypeStruct((n, rwkv_dim), F32)
        prep_out = pl.pallas_call(
            functools.partial(_prep_kernel, tiles_per_seq=seq // TM_PREP, conv_dim=conv_dim, rwkv_dim=rwkv_dim),
            grid=(n // TM_PREP,),
            in_specs=[
                row_spec(d_model), _const_spec((1, d_model)), _const_spec(win_p.shape),
                _const_spec((conv_w.shape[1], conv_dim)), _const_spec((1, rwkv_cols)),
                _const_spec(ww.shape), _const_spec((1, rwkv_dim)), _const_spec(wa.shape),
                _const_spec((1, rwkv_dim)), _const_spec(wg.shape), _const_spec((1, rwkv_dim)),
                _const_spec((1, rwkv_dim)), _const_spec(bd.shape)],
            out_specs=[row_spec(conv_dim)] + [row_spec(rwkv_dim)] * 7,
            out_shape=[jax.ShapeDtypeStruct((n, conv_dim), F32)] + [stream] * 7,
            scratch_shapes=[pltpu.VMEM((CARRY_ROWS, conv_dim), F32), pltpu.VMEM((CARRY_ROWS, rwkv_cols), F32)],
            compiler_params=pltpu.CompilerParams(dimension_semantics=("arbitrary",), vmem_limit_bytes=VMEM_LIMIT),
            name="prep",
        )(h, row(norm_mix_g[li]), win_p, conv_w[li], mu_p, ww, row(w0[li]), wa, row(a0[li]), wg,
          row(k_k[li]), row(k_a[li]), bd)
        y_conv = prep_out[0]
        streams = [a.reshape(bsz, seq, rwkv_dim) for a in prep_out[1:]]

        seq_spec = pl.BlockSpec((bsz, TM_RWKV, rwkv_dim), lambda j: (0, j, 0))
        y_rwkv = pl.pallas_call(
            _rwkv_kernel,
            grid=(seq // TM_RWKV,),
            in_specs=[seq_spec] * 7 + [
                _const_spec(tril.shape), _const_spec(bd.shape),
                _const_spec((1, rwkv_dim)), _const_spec((1, rwkv_dim)), _const_spec((1, rwkv_dim))],
            out_specs=seq_spec,
            out_shape=jax.ShapeDtypeStruct((bsz, seq, rwkv_dim), F32),
            scratch_shapes=[pltpu.VMEM((bsz, rwkv_dim // GROUP, HEAD_DIM, GROUP), F32)],
            compiler_params=pltpu.CompilerParams(dimension_semantics=("arbitrary",), vmem_limit_bytes=VMEM_LIMIT),
            name="rwkv",
        )(*streams, tril, bd, row(r_k[li]), row(ln_x_g[li]), row(ln_x_b[li]))
        y_rwkv = y_rwkv.reshape(n, rwkv_dim)

        tail_row = lambda w: pl.BlockSpec((TM_TAIL, w), lambda i: (i, 0))
        ple_dim = p.shape[-1]
        d_ff = w_up.shape[2]
        h = pl.pallas_call(
            functools.partial(_tail_kernel, conv_dim=conv_dim, ff_block=1024, final_norm=(li == depth - 1)),
            grid=(n // TM_TAIL,),
            in_specs=[
                tail_row(d_model), tail_row(conv_dim), tail_row(rwkv_dim), tail_row(ple_dim),
                _const_spec((d_model, d_model)), _const_spec((1, d_model)), _const_spec((d_model, d_ff)),
                _const_spec((d_ff, d_model)), _const_spec((1, d_model)), _const_spec((d_model, d_model)),
                _const_spec((ple_dim, d_model)), _const_spec((1, d_model))],
            out_specs=tail_row(d_model),
            out_shape=jax.ShapeDtypeStruct((n, d_model), F32),
            compiler_params=pltpu.CompilerParams(dimension_semantics=("parallel",), vmem_limit_bytes=VMEM_LIMIT),
            name="tail",
        )(h, y_conv, y_rwkv, p[li].reshape(n, ple_dim), w_out[li].astype(BF16), row(norm_mlp_g[li]),
          w_up[li].astype(BF16), w_down[li].astype(BF16), row(norm_ple_g[li]),
          w_ple_gate[li].astype(BF16), w_ple_proj[li].astype(BF16), row(norm_final_g))
    return h.reshape(bsz, seq, d_model)
```

```python
import functools
import math

import numpy as np
import jax
import jax.numpy as jnp
from jax import lax
from jax.experimental import pallas as pl
from jax.experimental.pallas import tpu as pltpu

F32 = jnp.float32
BF16 = jnp.bfloat16

HEAD_DIM = 64
CHUNK = 64
LANES = 128
GROUP_HEADS = 4
GROUP = GROUP_HEADS * HEAD_DIM
RMS_EPS = 1e-6
GN_EPS = 64e-5
L2_EPS = 1e-12
DECAY_SCALE = math.exp(-0.5)
CARRY_ROWS = 8

TM_PREP = 512
PREP_PARTS = 2
TM_RWKV = 256
INV_BASE = 16
PREFIX_WAVES = 1
TM_TAIL = 512
TAIL_PARTS = 2
VMEM_LIMIT = 56 * 1024 * 1024


def _bf(x):
    return x.astype(BF16)


def _split2(x):
    hi = x.astype(BF16)
    lo = (x - hi.astype(F32)).astype(BF16)
    return hi, lo


def _dot(a, b):
    return jnp.dot(a, b, preferred_element_type=F32)


def _dot_nt(a, b):
    return lax.dot_general(a, b, (((1,), (1,)), ((), ())), preferred_element_type=F32)


def _headsum(x, bd):
    parts = []
    for g in range(x.shape[1] // GROUP):
        xh, xl = _split2(x[:, g * GROUP:(g + 1) * GROUP])
        parts.append(_dot(xh, bd) + _dot(xl, bd))
    return parts[0] if len(parts) == 1 else jnp.concatenate(parts, axis=1)


def _rms(x, g):
    return x * lax.rsqrt(jnp.mean(x * x, axis=-1, keepdims=True) + RMS_EPS) * g


def _shift_rows(x, carry, n):
    rows = lax.broadcasted_iota(jnp.int32, x.shape, 0)
    out = pltpu.roll(x, n, 0)
    for q in range(n):
        src = CARRY_ROWS - n + q
        out = jnp.where(rows == q, carry[src:src + 1, :], out)
    return out


def _prep_kernel(x_ref, g_ref, win_ref, convw_ref, mu_ref, ww_ref, w0_ref, wa_ref,
                 a0_ref, wg_ref, kkw_ref, ka_ref, bd_ref,
                 yconv_ref, r_ref, k_ref, v_ref, lw_ref, kk_ref, bb_ref, gate_ref,
                 ucar_ref, ccar_ref, *, tiles_per_seq, conv_dim, rwkv_dim):
    i = pl.program_id(0)

    @pl.when(i % tiles_per_seq == 0)
    def _():
        ucar_ref[...] = jnp.zeros_like(ucar_ref)
        ccar_ref[...] = jnp.zeros_like(ccar_ref)

    tm = x_ref.shape[0]
    pr = tm // PREP_PARTS
    parts = [slice(q * pr, (q + 1) * pr) for q in range(PREP_PARTS)]
    c3 = 3 * conv_dim
    d = rwkv_dim
    cw = convw_ref[...]

    xb = [_bf(_rms(x_ref[rs, :], g_ref[...])) for rs in parts]
    pa = [_dot(xq, win_ref[:, 0:c3]) for xq in xb]
    pb = [_dot(xq, win_ref[:, c3:]) for xq in xb]

    ucar = ucar_ref[...]
    for q, rs in enumerate(parts):
        gate_b = pa[q][:, 0:conv_dim]
        u = pa[q][:, conv_dim:2 * conv_dim] * pa[q][:, 2 * conv_dim:c3]
        conv = (_shift_rows(u, ucar, 2) * cw[0:1, :] + _shift_rows(u, ucar, 1) * cw[1:2, :]
                + u * cw[2:3, :])
        yconv_ref[rs, :] = gate_b * conv
        ucar = u[pr - CARRY_ROWS:pr, :]
    ucar_ref[...] = ucar

    ccar = ccar_ref[...]
    uu = []
    for q in range(PREP_PARTS):
        prev = _shift_rows(pb[q], ccar, 1)
        uu.append(pb[q] + mu_ref[...] * (prev - pb[q]))
        ccar = pb[q][pr - CARRY_ROWS:pr, :]
    ccar_ref[...] = ccar

    z = [w0_ref[...] + _dot(_bf(jnp.tanh(uq[:, 3 * d:3 * d + LANES])), ww_ref[...]) for uq in uu]
    za = [a0_ref[...] + _dot(_bf(uq[:, 3 * d + LANES:3 * d + 2 * LANES]), wa_ref[...]) for uq in uu]
    gate = [_dot(_bf(jax.nn.sigmoid(uq[:, 3 * d + 2 * LANES:])), wg_ref[...]) for uq in uu]
    kk = [uq[:, d:2 * d] * kkw_ref[...] for uq in uu]
    ssq = [_headsum(kq * kq, bd_ref[...]) for kq in kk]

    for q, rs in enumerate(parts):
        iclr = jax.nn.sigmoid(za[q])
        kn = kk[q] / jnp.maximum(jnp.sqrt(ssq[q]), L2_EPS)
        r_ref[rs, :] = uu[q][:, 0:d]
        k_ref[rs, :] = uu[q][:, d:2 * d] * (1.0 + (iclr - 1.0) * ka_ref[...])
        v_ref[rs, :] = uu[q][:, 2 * d:3 * d]
        lw_ref[rs, :] = -DECAY_SCALE * jax.nn.sigmoid(z[q])
        kk_ref[rs, :] = kn
        bb_ref[rs, :] = kn * iclr
        gate_ref[rs, :] = gate[q]


def _rwkv_kernel(r_ref, k_ref, v_ref, lw_ref, kk_ref, bb_ref, gate_ref,
                 tril_ref, bd_ref, rk_ref, lng_ref, lnb_ref,
                 out_ref, s_ref):
    @pl.when(pl.program_id(0) == 0)
    def _():
        s_ref[...] = jnp.zeros_like(s_ref)

    n_seq, tm, dim = r_ref.shape
    n_groups = dim // GROUP
    n_chunks = tm // CHUNK

    ri = lax.broadcasted_iota(jnp.int32, (CHUNK, GROUP), 0)
    ji = lax.broadcasted_iota(jnp.int32, (CHUNK, GROUP), 1) % HEAD_DIM
    m_strict = ji < ri
    m_incl = ji <= ri
    eye = (ji == ri).astype(F32)
    same_base = (ji // INV_BASE) == (ri // INV_BASE)
    off_masks = [((ji // (2 * s)) == (ri // (2 * s))) & ((ji // s) != (ri // s))
                 for s in (INV_BASE, 2 * INV_BASE)]
    zeros_c = jnp.zeros((CHUNK, GROUP), F32)
    gr = lax.broadcasted_iota(jnp.int32, (GROUP, GROUP), 0) // HEAD_DIM
    gc = lax.broadcasted_iota(jnp.int32, (GROUP, GROUP), 1) // HEAD_DIM
    bd_mask = gr == gc
    zeros_gb = jnp.zeros((GROUP, GROUP), BF16)
    lane_lo = lax.broadcasted_iota(jnp.int32, (CHUNK, LANES), 1) < HEAD_DIM

    def expand(x):
        xb = _bf(x)
        return jnp.where(bd_mask, jnp.concatenate([xb] * GROUP_HEADS, axis=0), zeros_gb)

    def mul(x, y):
        return _dot(_bf(x), expand(y))

    def fold(full):
        cols = []
        for t in range(GROUP // LANES):
            top = full[(2 * t) * HEAD_DIM:(2 * t + 1) * HEAD_DIM, t * LANES:(t + 1) * LANES]
            bot = full[(2 * t + 1) * HEAD_DIM:(2 * t + 2) * HEAD_DIM, t * LANES:(t + 1) * LANES]
            cols.append(jnp.where(lane_lo, top, bot))
        return jnp.concatenate(cols, axis=1)

    tril = tril_ref[...]
    bd = bd_ref[...]
    rk = rk_ref[...]
    lng = lng_ref[...]
    lnb = lnb_ref[...]

    chains = {}
    for b in range(n_seq):
        lw = lw_ref[b]
        lh, ll = _split2(lw)
        c = _dot(tril, lh) + _dot(tril, ll)
        ctot = [c[ch * CHUNK + CHUNK - 1:(ch + 1) * CHUNK, :] for ch in range(n_chunks)]
        cl = jnp.concatenate([jnp.broadcast_to(ct, (CHUNK, dim)) for ct in ctot], axis=0)
        enc = jnp.exp(-c)
        ed = jnp.exp(cl - c)
        r = r_ref[b]
        kmod = k_ref[b]
        v = v_ref[b]
        bb = bb_ref[b]
        full = dict(
            at=_bf(-(kk_ref[b] * jnp.exp(c - lw))), rt=_bf(r * jnp.exp(c)),
            bt=_bf(bb * enc), kt=_bf(kmod * enc), bh=_bf(bb * ed), kh=_bf(kmod * ed),
            v=v, bonus=_headsum(r * kmod * rk, bd) * v)
        for ch in range(n_chunks):
            rows = slice(ch * CHUNK, (ch + 1) * CHUNK)
            for g in range(n_groups):
                lanes = slice(g * GROUP, (g + 1) * GROUP)
                cd = {name: val[rows, lanes] for name, val in full.items()}
                cd["decay"] = jnp.exp(ctot[ch][:, lanes])
                chains[(ch, b, g)] = cd

    keys = sorted(chains)
    t = {}
    wave_len = len(keys) // PREFIX_WAVES
    for wave in [keys[w * wave_len:(w + 1) * wave_len] for w in range(PREFIX_WAVES)]:
        gmat = {}
        for key in wave:
            cd = chains[key]
            lhs = _bf(jnp.concatenate([cd["at"], cd["rt"]], axis=0))
            rhs = jnp.concatenate([expand(cd["bt"]), expand(cd["kt"])], axis=0)
            gmat[key] = _dot_nt(lhs, rhs)
        for key in wave:
            cd, g = chains[key], gmat[key]
            cd["a_ab"] = jnp.where(m_strict, g[0:CHUNK, 0:GROUP], zeros_c)
            cd["a_rb"] = jnp.where(m_incl, g[CHUNK:, 0:GROUP], zeros_c)
            a_k = jnp.concatenate([jnp.where(m_strict, g[0:CHUNK, GROUP:], zeros_c),
                                   jnp.where(m_incl, g[CHUNK:, GROUP:], zeros_c)], axis=0)
            cd["akv_rkv"] = _dot(_bf(a_k), expand(cd["v"]))

        nd = {key: jnp.where(same_base, chains[key]["a_ab"], zeros_c) for key in wave}
        pw = {key: mul(nd[key], nd[key]) for key in wave}
        sm = {key: eye + nd[key] for key in wave}
        for _ in range(2):
            prod = {key: _dot(_bf(jnp.concatenate([sm[key], pw[key]], axis=0)), expand(pw[key]))
                    for key in wave}
            sm = {key: sm[key] + prod[key][0:CHUNK, :] for key in wave}
            pw = {key: prod[key][CHUNK:, :] for key in wave}
        prod = {key: mul(sm[key], pw[key]) for key in wave}
        tw = {key: sm[key] + prod[key] for key in wave}
        for m_off in off_masks:
            ct = {key: mul(jnp.where(m_off, chains[key]["a_ab"], zeros_c), tw[key]) for key in wave}
            prod = {key: mul(tw[key], ct[key]) for key in wave}
            tw = {key: tw[key] + prod[key] for key in wave}
        t.update(tw)

    state = {(b, g): s_ref[b, g] for b in range(n_seq) for g in range(n_groups)}
    inv_n = 1.0 / HEAD_DIM

    def chunk_keys(ch):
        return [key for key in keys if key[0] == ch]

    def norm_stage(ch, stage, carry):
        cks = chunk_keys(ch)
        if stage == 0:
            yy = jnp.concatenate([carry[key] for key in cks], axis=0)
            return yy, _headsum(yy, bd) * inv_n
        if stage == 1:
            yy, mu = carry
            yc = yy - mu
            return yc, _headsum(yc * yc, bd) * inv_n
        yc, var = carry
        yn = yc * lax.rsqrt(var + GN_EPS)
        for idx, (_, b, g) in enumerate(cks):
            rows = slice(ch * CHUNK, (ch + 1) * CHUNK)
            lanes = slice(g * GROUP, (g + 1) * GROUP)
            part = yn[idx * CHUNK:(idx + 1) * CHUNK, :] * lng[:, lanes] + lnb[:, lanes]
            out_ref[b, rows, lanes] = (part + chains[(ch, b, g)]["bonus"]) * gate_ref[b, rows, lanes]
        return None

    pending = None
    for ch in range(n_chunks + 1):
        cks = chunk_keys(ch) if ch < n_chunks else []
        x0 = {}
        for key in cks:
            cd = chains[key]
            xin = _bf(jnp.concatenate([cd["at"], cd["rt"]], axis=0))
            x0[key] = _dot_nt(xin, expand(state[key[1:]]))
        xk = {key: x0[key] + chains[key]["akv_rkv"] for key in cks}
        if ch > 0:
            pending = norm_stage(ch - 1, 0, pending)
        u = {key: mul(t[key], xk[key][0:CHUNK, :]) for key in cks}
        if ch > 0:
            pending = norm_stage(ch - 1, 1, pending)
        y = {key: xk[key][CHUNK:, :] + mul(chains[key]["a_rb"], u[key]) for key in cks}
        for key in cks:
            cd = chains[key]
            uv_t = jnp.concatenate([u[key], cd["v"]], axis=0).T
            full = _dot(_bf(uv_t), _bf(jnp.concatenate([cd["bh"], cd["kh"]], axis=0)))
            state[key[1:]] = state[key[1:]] * cd["decay"] + fold(full)
        if ch > 0:
            norm_stage(ch - 1, 2, pending)
        pending = y

    for (b, g), val in state.items():
        s_ref[b, g] = val


def _tail_kernel(x_ref, yc_ref, yr_ref, p_ref, wout_ref, gmlp_ref, wup_ref, wdown_ref,
                 gple_ref, wgate_ref, wple_ref, gfin_ref, out_ref, *, conv_dim, ff_block, final_norm):
    tm = x_ref.shape[0]
    parts = [slice(q * (tm // TAIL_PARTS), (q + 1) * (tm // TAIL_PARTS)) for q in range(TAIL_PARTS)]
    mix = [_dot(_bf(yc_ref[rs, :]), wout_ref[0:conv_dim, :]) + _dot(_bf(yr_ref[rs, :]), wout_ref[conv_dim:, :])
           for rs in parts]
    h = [x_ref[rs, :] + m for rs, m in zip(parts, mix)]

    hn = [_bf(_rms(hq, gmlp_ref[...])) for hq in h]
    d_ff = wup_ref.shape[1]
    acc = [None] * TAIL_PARTS
    for j in range(d_ff // ff_block):
        cols = slice(j * ff_block, (j + 1) * ff_block)
        for q in range(TAIL_PARTS):
            hid = jnp.maximum(_dot(hn[q], wup_ref[:, cols]), 0.0)
            part = _dot(_bf(hid * hid), wdown_ref[cols, :])
            acc[q] = part if acc[q] is None else acc[q] + part
    h = [hq + aq for hq, aq in zip(h, acc)]

    gate = [jax.nn.sigmoid(_dot(_bf(_rms(hq, gple_ref[...])), wgate_ref[...])) for hq in h]
    ple = [_dot(_bf(p_ref[rs, :]), wple_ref[...]) for rs in parts]
    for q, rs in enumerate(parts):
        hq = h[q] + gate[q] * ple[q]
        if final_norm:
            hq = _rms(hq, gfin_ref[...])
        out_ref[rs, :] = hq


def _const_spec(shape):
    nd = len(shape)
    return pl.BlockSpec(shape, lambda *_: (0,) * nd, pipeline_mode=pl.Buffered(1))


def _pad_cols(w, n):
    return jnp.pad(w, ((0, 0), (0, n - w.shape[1])))


def _pad_rows(w, n):
    return jnp.pad(w, ((0, n - w.shape[0]), (0, 0)))


def _block_diag_ones(n, blk):
    idx = np.arange(n) // blk
    return jnp.asarray(idx[:, None] == idx[None, :], dtype=BF16)


def _chunk_tril(n, blk):
    t = np.arange(n)
    same = (t[:, None] // blk) == (t[None, :] // blk)
    return jnp.asarray(same & (t[None, :] <= t[:, None]), dtype=BF16)


def kernel(x, p, norm_mix_g, w_in, conv_w, shift_mu, w_lora_up, w0, a_lora_up, a0, g_lora_up, k_k, k_a, r_k, ln_x_g, ln_x_b, w_out, norm_mlp_g, w_up, w_down, norm_ple_g, w_ple_gate, w_ple_proj, norm_final_g):
    bsz, seq, d_model = x.shape
    depth = w_in.shape[0]
    n = bsz * seq
    rwkv_dim = w_lora_up.shape[2]
    conv_dim = conv_w.shape[2]
    decay_rank = w_lora_up.shape[1]
    iclr_rank = a_lora_up.shape[1]
    gate_rank = g_lora_up.shape[1]
    c3 = 3 * conv_dim
    r3 = c3 + 3 * rwkv_dim
    assert seq % TM_PREP == 0 and seq % TM_RWKV == 0 and n % TM_TAIL == 0 and TM_RWKV % CHUNK == 0
    assert rwkv_dim % GROUP == 0 and decay_rank <= LANES and iclr_rank <= LANES and gate_rank <= 2 * LANES
    rwkv_cols = 3 * rwkv_dim + 4 * LANES

    bd = _block_diag_ones(GROUP, HEAD_DIM)
    tril = _chunk_tril(TM_RWKV, CHUNK)
    row = lambda a: a.reshape(1, -1).astype(F32)

    h = x.reshape(n, d_model)
    for li in range(depth):
        wi = w_in[li]
        s0, s1, s2 = r3, r3 + decay_rank, r3 + decay_rank + iclr_rank
        win_p = jnp.concatenate([
            wi[:, :s0], _pad_cols(wi[:, s0:s1], LANES), _pad_cols(wi[:, s1:s2], LANES),
            _pad_cols(wi[:, s2:], 2 * LANES)], axis=1).astype(BF16)
        mu = shift_mu[li].reshape(1, -1)
        o = 3 * rwkv_dim
        mu_p = jnp.concatenate([
            mu[:, :o], _pad_cols(mu[:, o:o + decay_rank], LANES),
            _pad_cols(mu[:, o + decay_rank:o + decay_rank + iclr_rank], LANES),
            _pad_cols(mu[:, o + decay_rank + iclr_rank:], 2 * LANES)], axis=1)
        ww = _pad_rows(w_lora_up[li], LANES).astype(BF16)
        wa = _pad_rows(a_lora_up[li], LANES).astype(BF16)
        wg = _pad_rows(g_lora_up[li], 2 * LANES).astype(BF16)

        row_spec = lambda w: pl.BlockSpec((TM_PREP, w), lambda i: (i, 0))
        stream = jax.ShapeDtypeStruct((n, rwkv_dim), F32)
        prep_out = pl.pallas_call(
            functools.partial(_prep_kernel, tiles_per_seq=seq // TM_PREP, conv_dim=conv_dim, rwkv_dim=rwkv_dim),
            grid=(n // TM_PREP,),
            in_specs=[
                row_spec(d_model), _const_spec((1, d_model)), _const_spec(win_p.shape),
                _const_spec((conv_w.shape[1], conv_dim)), _const_spec((1, rwkv_cols)),
                _const_spec(ww.shape), _const_spec((1, rwkv_dim)), _const_spec(wa.shape),
                _const_spec((1, rwkv_dim)), _const_spec(wg.shape), _const_spec((1, rwkv_dim)),
                _const_spec((1, rwkv_dim)), _const_spec(bd.shape)],
            out_specs=[row_spec(conv_dim)] + [row_spec(rwkv_dim)] * 7,
            out_shape=[jax.ShapeDtypeStruct((n, conv_dim), F32)] + [stream] * 7,
            scratch_shapes=[pltpu.VMEM((CARRY_ROWS, conv_dim), F32), pltpu.VMEM((CARRY_ROWS, rwkv_cols), F32)],
            compiler_params=pltpu.CompilerParams(dimension_semantics=("arbitrary",), vmem_limit_bytes=VMEM_LIMIT),
            name="prep",
        )(h, row(norm_mix_g[li]), win_p, conv_w[li], mu_p, ww, row(w0[li]), wa, row(a0[li]), wg,
          row(k_k[li]), row(k_a[li]), bd)
        y_conv = prep_out[0]
        streams = [a.reshape(bsz, seq, rwkv_dim) for a in prep_out[1:]]

        seq_spec = pl.BlockSpec((bsz, TM_RWKV, rwkv_dim), lambda j: (0, j, 0))
        y_rwkv = pl.pallas_call(
            _rwkv_kernel,
            grid=(seq // TM_RWKV,),
            in_specs=[seq_spec] * 7 + [
                _const_spec(tril.shape), _const_spec(bd.shape),
                _const_spec((1, rwkv_dim)), _const_spec((1, rwkv_dim)), _const_spec((1, rwkv_dim))],
            out_specs=seq_spec,
            out_shape=jax.ShapeDtypeStruct((bsz, seq, rwkv_dim), F32),
            scratch_shapes=[pltpu.VMEM((bsz, rwkv_dim // GROUP, HEAD_DIM, GROUP), F32)],
            compiler_params=pltpu.CompilerParams(dimension_semantics=("arbitrary",), vmem_limit_bytes=VMEM_LIMIT),
            name="rwkv",
        )(*streams, tril, bd, row(r_k[li]), row(ln_x_g[li]), row(ln_x_b[li]))
        y_rwkv = y_rwkv.reshape(n, rwkv_dim)

        tail_row = lambda w: pl.BlockSpec((TM_TAIL, w), lambda i: (i, 0))
        ple_dim = p.shape[-1]
        d_ff = w_up.shape[2]
        h = pl.pallas_call(
            functools.partial(_tail_kernel, conv_dim=conv_dim, ff_block=1024, final_norm=(li == depth - 1)),
            grid=(n // TM_TAIL,),
            in_specs=[
                tail_row(d_model), tail_row(conv_dim), tail_row(rwkv_dim), tail_row(ple_dim),
                _const_spec((d_model, d_model)), _const_spec((1, d_model)), _const_spec((d_model, d_ff)),
                _const_spec((d_ff, d_model)), _const_spec((1, d_model)), _const_spec((d_model, d_model)),
                _const_spec((ple_dim, d_model)), _const_spec((1, d_model))],
            out_specs=tail_row(d_model),
            out_shape=jax.ShapeDtypeStruct((n, d_model), F32),
            compiler_params=pltpu.CompilerParams(dimension_semantics=("parallel",), vmem_limit_bytes=VMEM_LIMIT),
            name="tail",
        )(h, y_conv, y_rwkv, p[li].reshape(n, ple_dim), w_out[li].astype(BF16), row(norm_mlp_g[li]),
          w_up[li].astype(BF16), w_down[li].astype(BF16), row(norm_ple_g[li]),
          w_ple_gate[li].astype(BF16), w_ple_proj[li].astype(BF16), row(norm_final_g))
    return h.reshape(bsz, seq, d_model)
```
